```python
import jax, jax.numpy as jnp
from jax import lax
import numpy as np

D_MODEL = 1024
BATCH = 8
SEQ = 2048
DEPTH = 4
DEC_BATCH = 128
DEC_SEQ = 1
PAST_LEN = 16384
PAGE_SIZE = 128

H_A = 4
DK = D_MODEL // 8
DV = D_MODEL // 8
QK_DIM = H_A * DK
V_DIM = H_A * DV
QKV_DIM = 2 * QK_DIM + V_DIM
CONV_A = 4
DELTA_CHUNK = 64
H_B = 4
D_B = D_MODEL // 8
SGU_DIM = H_B * D_B
SGU_CHUNK = 128
OFF_Z = QKV_DIM
OFF_BETA = OFF_Z + V_DIM
OFF_ALPHA = OFF_BETA + H_A
OFF_U = OFF_ALPHA + H_A
OFF_VB = OFF_U + SGU_DIM
IN_DIM = OFF_VB + SGU_DIM
MIX_DIM = V_DIM + SGU_DIM
D_FF = ((8 * D_MODEL // 3 + 127) // 128) * 128
CONV_F = 3
N_MOD = 6
EPS = 1e-6

kernel_name = "hymba_gdn_chunkmlp_convffn_adaln_step"


def rms_norm(x, w):
    xf = x.astype(jnp.float32)
    y = xf * lax.rsqrt(jnp.mean(xf * xf, axis=-1, keepdims=True) + EPS)
    return y * w.astype(jnp.float32)


def l2_normalize(x):
    xf = x.astype(jnp.float32)
    return xf * lax.rsqrt(jnp.sum(xf * xf, axis=-1, keepdims=True) + EPS)


def causal_dwconv(x, prev, w):
    width = w.shape[0]
    T = x.shape[1]
    xp = jnp.concatenate([prev.astype(x.dtype), x], axis=1)
    y = xp[:, 0:T] * w[0]
    for j in range(1, width):
        y = y + xp[:, j:j + T] * w[j]
    return y, xp[:, T:]


def gated_delta_rule(q, k, v, g, beta, s0):
    nb, T = q.shape[0], q.shape[1]
    C = DELTA_CHUNK
    pad = (-T) % C
    n = (T + pad) // C
    q = l2_normalize(q) * (DK ** -0.5)
    k = l2_normalize(k)
    v = v.astype(jnp.float32)
    g = g.astype(jnp.float32)
    beta = beta.astype(jnp.float32)

    def to_chunks(a):
        a = jnp.pad(a, [(0, 0), (0, pad)] + [(0, 0)] * (a.ndim - 2))
        a = a.reshape((nb, n, C) + a.shape[2:])
        a = jnp.moveaxis(a, 3, 2)
        return jnp.moveaxis(a, 1, 0)

    qc, kc, vc, gc, bc = (to_chunks(a) for a in (q, k, v, g, beta))
    gcum = jnp.cumsum(gc, axis=-1)
    tril = jnp.tril(jnp.ones((C, C), dtype=bool))
    strict = jnp.tril(jnp.ones((C, C), dtype=bool), -1)
    decay = jnp.exp(jnp.where(tril, gcum[..., :, None] - gcum[..., None, :], -jnp.inf))
    kb = kc * bc[..., None]
    vb = vc * bc[..., None]
    lower = jnp.where(strict, jnp.einsum('nbhid,nbhjd->nbhij', kb, kc) * decay, 0.0)
    a_mat = lower + jnp.eye(C, dtype=jnp.float32)
    rhs = jnp.concatenate([vb, kb * jnp.exp(gcum)[..., None]], axis=-1)
    sol = lax.linalg.triangular_solve(a_mat, rhs, left_side=True, lower=True, unit_diagonal=True)
    u_c = sol[..., :DV]
    w_c = sol[..., DV:]
    qk = jnp.where(tril, jnp.einsum('nbhid,nbhjd->nbhij', qc, kc) * decay, 0.0)

    def step(S, xs):
        q_i, k_i, u_i, w_i, qk_i, g_i = xs
        v_new = u_i - jnp.einsum('bhck,bhkv->bhcv', w_i, S)
        o = (jnp.einsum('bhck,bhkv->bhcv', q_i * jnp.exp(g_i)[..., None], S)
             + jnp.einsum('bhij,bhjv->bhiv', qk_i, v_new))
        g_last = g_i[..., -1]
        k_dec = k_i * jnp.exp(g_last[..., None] - g_i)[..., None]
        S = S * jnp.exp(g_last)[..., None, None] + jnp.einsum('bhck,bhcv->bhkv', k_dec, v_new)
        return S, o

    S, o = lax.scan(step, s0.astype(jnp.float32), (qc, kc, u_c, w_c, qk, gcum))
    o = jnp.transpose(o, (1, 0, 3, 2, 4)).reshape(nb, n * C, H_A, DV)[:, :T]
    return o, S


def chunk_spatial_gating(u, v, w_s, b_s):
    nb, T = v.shape[0], v.shape[1]
    C = SGU_CHUNK
    pad = (-T) % C
    n = (T + pad) // C
    vp = jnp.pad(v, ((0, 0), (0, pad), (0, 0), (0, 0))).reshape(nb, n, C, H_B, D_B)
    w = jnp.where(jnp.tril(jnp.ones((C, C), dtype=bool)), w_s, 0.0)
    z = jnp.einsum('hij,bnjhd->bnihd', w, vp) + b_s.T[None, None, :, :, None]
    z = z.reshape(nb, n * C, H_B, D_B)[:, :T]
    return u * z


def run_trunk(x, c, s_delta, s_qkv, s_ffn, w_mod, b_mod, norm1, w_in, conv_qkv, a_log,
              dt_bias, gdn_norm, sgu_norm, w_sgu, b_sgu, w_out, norm2, w_up, conv_ffn_w,
              conv_ffn_b, w_down, final_norm):
    nb, T = x.shape[0], x.shape[1]
    new_S, new_qkv, new_ffn, v_rows = [], [], [], []
    c_act = jax.nn.silu(c)
    for l in range(DEPTH):
        mod = (c_act @ w_mod[l] + b_mod[l])[:, None, :]
        sh1, sc1, g1, sh2, sc2, g2 = jnp.split(mod, N_MOD, axis=-1)
        h = (rms_norm(x, norm1[l]) * (1.0 + sc1) + sh1).astype(x.dtype)
        proj = h @ w_in[l]
        qkv = proj[..., :OFF_Z]
        z = proj[..., OFF_Z:OFF_BETA].reshape(nb, T, H_A, DV)
        b_raw = proj[..., OFF_BETA:OFF_ALPHA]
        a_raw = proj[..., OFF_ALPHA:OFF_U]
        u_raw = proj[..., OFF_U:OFF_VB]
        vb_raw = proj[..., OFF_VB:]
        qkv_c, qkv_buf = causal_dwconv(qkv, s_qkv[l], conv_qkv[l])
        qkv_c = jax.nn.silu(qkv_c)
        q = qkv_c[..., :QK_DIM].reshape(nb, T, H_A, DK)
        k = qkv_c[..., QK_DIM:2 * QK_DIM].reshape(nb, T, H_A, DK)
        v = qkv_c[..., 2 * QK_DIM:].reshape(nb, T, H_A, DV)
        beta = jax.nn.sigmoid(b_raw.astype(jnp.float32))
        g = -jnp.exp(a_log[l].astype(jnp.float32)) * jax.nn.softplus(
            a_raw.astype(jnp.float32) + dt_bias[l].astype(jnp.float32))
        o_a, S = gated_delta_rule(q, k, v, g, beta, s_delta[l])
        o_a = rms_norm(o_a, gdn_norm[l]) * jax.nn.silu(z.astype(jnp.float32))
        u = jax.nn.gelu(u_raw.astype(jnp.float32)).reshape(nb, T, H_B, D_B)
        vv = rms_norm(jax.nn.gelu(vb_raw.astype(jnp.float32)).reshape(nb, T, H_B, D_B), sgu_norm[l])
        o_b = chunk_spatial_gating(u, vv, w_sgu[l].astype(jnp.float32), b_sgu[l].astype(jnp.float32))
        mix = jnp.concatenate([o_a.reshape(nb, T, V_DIM), o_b.reshape(nb, T, SGU_DIM)],
                              axis=-1).astype(x.dtype)
        x = x + g1 * (mix @ w_out[l])
        h2 = (rms_norm(x, norm2[l]) * (1.0 + sc2) + sh2).astype(x.dtype)
        up = h2 @ w_up[l]
        up_c, ffn_buf = causal_dwconv(up, s_ffn[l], conv_ffn_w[l])
        up_c = up_c + conv_ffn_b[l]
        x = x + g2 * ((jax.nn.silu(up_c[..., :D_FF]) * up_c[..., D_FF:]) @ w_down[l])
        new_S.append(S.astype(s_delta.dtype))
        new_qkv.append(qkv_buf)
        new_ffn.append(ffn_buf)
        v_rows.append(vv.astype(x.dtype))
    y = rms_norm(x, final_norm).astype(x.dtype)
    return y, jnp.stack(new_S), jnp.stack(new_qkv), jnp.stack(new_ffn), v_rows


def setup_inputs(seed: int = 0) -> dict:
    key = jax.random.key(seed)
    ks = jax.random.split(key, 32)
    nrm = lambda k, s, sc: jax.random.normal(k, s, jnp.float32) * sc
    dt = jnp.exp(jax.random.uniform(ks[12], (DEPTH, H_A), jnp.float32,
                                    np.log(1e-3).astype(np.float32), np.log(1e-1).astype(np.float32)))
    return {
        "x_prompt": nrm(ks[0], (BATCH, SEQ, D_MODEL), 1.0),
        "x_sample": nrm(ks[1], (DEC_BATCH, DEC_SEQ, D_MODEL), 1.0),
        "state_delta": nrm(ks[2], (DEPTH, DEC_BATCH, H_A, DK, DV), 0.5),
        "state_qkv_conv": nrm(ks[3], (DEPTH, DEC_BATCH, CONV_A - 1, QKV_DIM), 1.0),
        "state_ffn_conv": nrm(ks[4], (DEPTH, DEC_BATCH, CONV_F - 1, 2 * D_FF), 1.0),
        "c_prompt": nrm(ks[5], (BATCH, D_MODEL), 1.0),
        "c_sample": nrm(ks[6], (DEC_BATCH, D_MODEL), 1.0),
        "w_mod": nrm(ks[7], (DEPTH, D_MODEL, N_MOD * D_MODEL), 0.02),
        "b_mod": nrm(ks[8], (DEPTH, N_MOD * D_MODEL), 0.02),
        "norm1": 1.0 + nrm(ks[9], (DEPTH, D_MODEL), 0.02),
        "w_in": nrm(ks[10], (DEPTH, D_MODEL, IN_DIM), D_MODEL ** -0.5),
        "conv_qkv": nrm(ks[11], (DEPTH, CONV_A, QKV_DIM), CONV_A ** -0.5),
        "a_log": jnp.log(jax.random.uniform(ks[13], (DEPTH, H_A), jnp.float32, 1.0, 16.0)),
        "dt_bias": dt + jnp.log(-jnp.expm1(-dt)),
        "gdn_norm": 1.0 + nrm(ks[14], (DEPTH, DV), 0.02),
        "sgu_norm": 1.0 + nrm(ks[15], (DEPTH, H_B, D_B), 0.02),
        "w_sgu": nrm(ks[16], (DEPTH, H_B, SGU_CHUNK, SGU_CHUNK), SGU_CHUNK ** -0.5),
        "b_sgu": 1.0 + nrm(ks[17], (DEPTH, H_B, SGU_CHUNK), 0.02),
        "w_out": nrm(ks[18], (DEPTH, MIX_DIM, D_MODEL), MIX_DIM ** -0.5),
        "norm2": 1.0 + nrm(ks[19], (DEPTH, D_MODEL), 0.02),
        "w_up": nrm(ks[20], (DEPTH, D_MODEL, 2 * D_FF), D_MODEL ** -0.5),
        "conv_ffn_w": nrm(ks[21], (DEPTH, CONV_F, 2 * D_FF), CONV_F ** -0.5),
        "conv_ffn_b": nrm(ks[22], (DEPTH, 2 * D_FF), 0.02),
        "w_down": nrm(ks[23], (DEPTH, D_FF, D_MODEL), D_FF ** -0.5),
        "final_norm": 1.0 + nrm(ks[24], (D_MODEL,), 0.02),
    }


def reference(x_prompt, x_sample, state_delta, state_qkv_conv, state_ffn_conv, c_prompt, c_sample,
              w_mod, b_mod, norm1, w_in, conv_qkv, a_log, dt_bias, gdn_norm, sgu_norm, w_sgu, b_sgu,
              w_out, norm2, w_up, conv_ffn_w, conv_ffn_b, w_down, final_norm):
    weights = (w_mod, b_mod, norm1, w_in, conv_qkv, a_log, dt_bias, gdn_norm, sgu_norm, w_sgu,
               b_sgu, w_out, norm2, w_up, conv_ffn_w, conv_ffn_b, w_down, final_norm)
    nbp = x_prompt.shape[0]
    s0_delta = jnp.zeros((DEPTH, nbp, H_A, DK, DV), state_delta.dtype)
    s0_qkv = jnp.zeros((DEPTH, nbp, CONV_A - 1, QKV_DIM), x_prompt.dtype)
    s0_ffn = jnp.zeros((DEPTH, nbp, CONV_F - 1, 2 * D_FF), x_prompt.dtype)
    y_prompt, delta_p, qkv_p, ffn_p, _ = run_trunk(x_prompt, c_prompt, s0_delta, s0_qkv, s0_ffn, *weights)
    y_sample, delta_s, qkv_s, ffn_s, v_rows_s = run_trunk(
        x_sample, c_sample, state_delta, state_qkv_conv, state_ffn_conv, *weights)
    sgu_v_s = jnp.stack(v_rows_s)
    return (y_prompt, y_sample, delta_p, delta_s, qkv_p, qkv_s, ffn_p, ffn_s, sgu_v_s)
```

```python
import functools
import math

import jax
import jax.numpy as jnp
from jax import lax
from jax.experimental import pallas as pl
from jax.experimental.pallas import tpu as pltpu

F32 = jnp.float32
BF16 = jnp.bfloat16

D_MODEL = 1024
DEPTH = 4
H_A = 4
DK = 128
DV = 128
QK_DIM = H_A * DK
V_DIM = H_A * DV
QKV_DIM = 2 * QK_DIM + V_DIM
CONV_A = 4
H_B = 4
D_B = 128
SGU_DIM = H_B * D_B
SGU_CHUNK = 128
D_FF = 2816
CONV_F = 3
N_MOD = 6
EPS = 1e-6

P_Z = QKV_DIM
P_U = P_Z + V_DIM
P_VB = P_U + SGU_DIM
P_SM = P_VB + SGU_DIM
P_DIM = P_SM + 128

LANES = 128
SUBLANES = 8
SUPER = 128
DELTA_C = 64
TB_MIX = 256
TB_FFN = 256
BB_STATE = 8
VMEM_LIMIT = 56 * 1024 * 1024


def _dot(a, b):
    return jnp.dot(a.astype(BF16), b.astype(BF16), preferred_element_type=F32)


def _dot_nt(a, b):
    return lax.dot_general(a.astype(BF16), b.astype(BF16), (((1,), (1,)), ((), ())),
                           preferred_element_type=F32)


def _dot_tn(a, b):
    return lax.dot_general(a.astype(BF16), b.astype(BF16), (((0,), (0,)), ((), ())),
                           preferred_element_type=F32)


def _rms(x, w):
    return x * lax.rsqrt(jnp.mean(x * x, axis=-1, keepdims=True) + EPS) * w


def _l2n(x):
    return x * lax.rsqrt(jnp.sum(x * x, axis=-1, keepdims=True) + EPS)


def _lane_col(x, lane):
    ids = lax.broadcasted_iota(jnp.int32, x.shape, 1)
    return jnp.sum(jnp.where(ids == lane, x, 0.0), axis=1, keepdims=True)


def _decay_logits(sm, alog_row, dtb_row):
    return -jnp.exp(alog_row) * jax.nn.softplus(sm + dtb_row)


def _const_spec(shape):
    nd = len(shape)
    return pl.BlockSpec(shape, lambda *_: (0,) * nd, pipeline_mode=pl.Buffered(1))


def _mod_kernel(c_ref, w_ref, b_ref, o_ref):
    c_act = jax.nn.silu(c_ref[...])
    o_ref[0] = _dot(c_act, w_ref[0]) + b_ref[0]


def _modulation(c_all, w_mod, b_mod):
    n = c_all.shape[0]
    tn = 1536
    return pl.pallas_call(
        _mod_kernel,
        grid=(DEPTH, (N_MOD * D_MODEL) // tn),
        in_specs=[
            pl.BlockSpec((n, D_MODEL), lambda l, j: (0, 0)),
            pl.BlockSpec((1, D_MODEL, tn), lambda l, j: (l, 0, j)),
            pl.BlockSpec((1, 1, tn), lambda l, j: (l, 0, j)),
        ],
        out_specs=pl.BlockSpec((1, n, tn), lambda l, j: (l, 0, j)),
        out_shape=jax.ShapeDtypeStruct((DEPTH, n, N_MOD * D_MODEL), F32),
        compiler_params=pltpu.CompilerParams(
            dimension_semantics=("arbitrary", "arbitrary"), vmem_limit_bytes=VMEM_LIMIT),
        name="modulation",
    )(c_all, w_mod, b_mod.reshape(DEPTH, 1, N_MOD * D_MODEL))


def _mixer_kernel(x_ref, mod_ref, norm1_ref, w_in_ref, convw_ref, alog_ref, dtb_ref,
                  gdn_ref, sgun_ref, wsgu_ref, bsgu_ref, w_out_ref,
                  xo_ref, s_out_ref, qkvbuf_ref,
                  xp_scr, s_scr, *, tb):
    t = pl.program_id(1)
    nt = pl.num_programs(1)

    @pl.when(t == 0)
    def _():
        xp_scr[0:SUBLANES, :] = jnp.zeros((SUBLANES, QKV_DIM), F32)
        s_scr[...] = jnp.zeros(s_scr.shape, F32)

    x = x_ref[0]
    sh1 = mod_ref[0, :, 0:D_MODEL]
    sc1 = mod_ref[0, :, D_MODEL:2 * D_MODEL]
    g1 = mod_ref[0, :, 2 * D_MODEL:3 * D_MODEL]
    h = _rms(x, norm1_ref[...]) * (1.0 + sc1) + sh1
    proj = _dot(h, w_in_ref[...])

    qkv = proj[:, 0:QKV_DIM]
    xp_scr[SUBLANES:SUBLANES + tb, :] = qkv
    acc = qkv * convw_ref[3:4, :]
    for j in range(CONV_A - 1):
        off = SUBLANES - (CONV_A - 1) + j
        acc = acc + xp_scr[off:off + tb, :] * convw_ref[j:j + 1, :]
    qkvbuf_ref[0] = xp_scr[tb + SUBLANES - (CONV_A - 1):tb + SUBLANES, :]
    xp_scr[0:SUBLANES, :] = xp_scr[tb:tb + SUBLANES, :]
    qkvc = jax.nn.silu(acc)

    sm = proj[:, P_SM:P_SM + LANES]
    beta_all = jax.nn.sigmoid(sm)
    g_all = _decay_logits(sm, alog_ref[...], dtb_ref[...])

    ri = lax.broadcasted_iota(jnp.int32, (tb, tb), 0)
    ci = lax.broadcasted_iota(jnp.int32, (tb, tb), 1)
    cum_mat = jnp.where((ri // DELTA_C == ci // DELTA_C) & (ri >= ci), 1.0, 0.0).astype(BF16)
    g_hi = g_all.astype(BF16)
    r1 = g_all - g_hi.astype(F32)
    g_mid = r1.astype(BF16)
    g_lo = (r1 - g_mid.astype(F32)).astype(BF16)
    gc_all = (jnp.dot(cum_mat, g_hi, preferred_element_type=F32)
              + jnp.dot(cum_mat, g_mid, preferred_element_type=F32)
              + jnp.dot(cum_mat, g_lo, preferred_element_type=F32))
    gc_t = gc_all.T

    rs = lax.broadcasted_iota(jnp.int32, (SUPER, SUPER), 0)
    cs = lax.broadcasted_iota(jnp.int32, (SUPER, SUPER), 1)
    same_chunk = (rs // DELTA_C) == (cs // DELTA_C)
    tril = same_chunk & (rs >= cs)
    strict = same_chunk & (rs > cs)
    eye = jnp.where(rs == cs, 1.0, 0.0)
    pair_mask = (rs // 2 == cs // 2) & (rs > cs)
    quad_masks = []
    half = 2
    while half < DELTA_C:
        quad_masks.append((rs // (2 * half) == cs // (2 * half))
                          & (rs % (2 * half) >= half) & (cs % (2 * half) < half))
        half *= 2
    chunks_per_super = SUPER // DELTA_C

    o_heads = []
    for hh in range(H_A):
        q_h = _l2n(qkvc[:, hh * DK:(hh + 1) * DK]) * (DK ** -0.5)
        k_h = _l2n(qkvc[:, QK_DIM + hh * DK:QK_DIM + (hh + 1) * DK])
        v_h = qkvc[:, 2 * QK_DIM + hh * DV:2 * QK_DIM + (hh + 1) * DV]
        beta_c = _lane_col(beta_all, hh)
        gc_c = _lane_col(gc_all, H_A + hh)
        o_rows = []
        for s in range(tb // SUPER):
            r0 = s * SUPER
            qs = q_h[r0:r0 + SUPER]
            ks = k_h[r0:r0 + SUPER]
            vs = v_h[r0:r0 + SUPER]
            bs = beta_c[r0:r0 + SUPER]
            gcol = gc_c[r0:r0 + SUPER]
            grow = gc_t[H_A + hh:H_A + hh + 1, r0:r0 + SUPER]
            decay = jnp.exp(jnp.where(tril, gcol - grow, -jnp.inf))
            kb = ks * bs
            low = jnp.where(strict, _dot_nt(kb, ks) * decay, 0.0)
            inv = eye - jnp.where(pair_mask, low, 0.0)
            for qm in quad_masks:
                inv = inv - _dot(_dot(inv, jnp.where(qm, low, 0.0)), inv)
            rhs = jnp.concatenate([vs * bs, kb * jnp.exp(gcol)], axis=1)
            sol = _dot(inv, rhs)
            u_s = sol[:, 0:DV]
            w_s = sol[:, DV:DV + DK]
            qk = jnp.where(tril, _dot_nt(qs, ks) * decay, 0.0)
            qg = qs * jnp.exp(gcol)
            for c in range(chunks_per_super):
                c0 = c * DELTA_C
                state = s_scr[hh]
                state_b = state.astype(BF16)
                v_new = u_s[c0:c0 + DELTA_C] - _dot(w_s[c0:c0 + DELTA_C], state_b)
                pieces = [jnp.zeros((DELTA_C, DV), F32)] * chunks_per_super
                pieces[c] = v_new
                v_full = pieces[0] if chunks_per_super == 1 else jnp.concatenate(pieces, axis=0)
                o_c = (_dot(qg[c0:c0 + DELTA_C], state_b)
                       + _dot(qk[c0:c0 + DELTA_C], v_full))
                g_last = grow[:, c0 + DELTA_C - 1:c0 + DELTA_C]
                k_dec = ks[c0:c0 + DELTA_C] * jnp.exp(g_last - gcol[c0:c0 + DELTA_C])
                s_scr[hh] = state * jnp.exp(g_last) + _dot_tn(k_dec, v_new)
                o_rows.append(o_c)
        o_h = jnp.concatenate(o_rows, axis=0)
        z_h = proj[:, P_Z + hh * DV:P_Z + (hh + 1) * DV]
        o_heads.append(_rms(o_h, gdn_ref[...]) * jax.nn.silu(z_h))

    rc = lax.broadcasted_iota(jnp.int32, (SGU_CHUNK, SGU_CHUNK), 0)
    cc = lax.broadcasted_iota(jnp.int32, (SGU_CHUNK, SGU_CHUNK), 1)
    for hh in range(H_B):
        u_h = jax.nn.gelu(proj[:, P_U + hh * D_B:P_U + (hh + 1) * D_B])
        v_h = _rms(jax.nn.gelu(proj[:, P_VB + hh * D_B:P_VB + (hh + 1) * D_B]),
                   sgun_ref[:, hh * D_B:(hh + 1) * D_B])
        w_h = jnp.where(rc >= cc, wsgu_ref[hh], 0.0)
        b_col = _lane_col(bsgu_ref[...], hh)
        z_rows = []
        for s in range(tb // SGU_CHUNK):
            r0 = s * SGU_CHUNK
            z_rows.append(_dot(w_h, v_h[r0:r0 + SGU_CHUNK]) + b_col)
        o_heads.append(u_h * jnp.concatenate(z_rows, axis=0))

    mix = jnp.concatenate(o_heads, axis=1)
    xo_ref[0] = x + g1 * _dot(mix, w_out_ref[...])

    @pl.when(t == nt - 1)
    def _():
        s_out_ref[0] = s_scr[...]


def _prompt_mixer(x, mod, norm1, w_in, convw, alog_row, dtb_row, gdn, sgun, wsgu, bsgu_t, w_out):
    nb, seq, _ = x.shape
    tb = TB_MIX
    kern = functools.partial(_mixer_kernel, tb=tb)
    return pl.pallas_call(
        kern,
        grid=(nb, seq // tb),
        in_specs=[
            pl.BlockSpec((1, tb, D_MODEL), lambda b, t: (b, t, 0)),
            pl.BlockSpec((1, 1, N_MOD * D_MODEL), lambda b, t: (b, 0, 0)),
            _const_spec((1, D_MODEL)),
            _const_spec((D_MODEL, P_DIM)),
            _const_spec((CONV_A, QKV_DIM)),
            _const_spec((1, LANES)),
            _const_spec((1, LANES)),
            _const_spec((1, DV)),
            _const_spec((1, SGU_DIM)),
            _const_spec((H_B, SGU_CHUNK, SGU_CHUNK)),
            _const_spec((SGU_CHUNK, LANES)),
            _const_spec((D_MODEL, D_MODEL)),
        ],
        out_specs=[
            pl.BlockSpec((1, tb, D_MODEL), lambda b, t: (b, t, 0)),
            pl.BlockSpec((1, H_A, DK, DV), lambda b, t: (b, 0, 0, 0)),
            pl.BlockSpec((1, CONV_A - 1, QKV_DIM), lambda b, t: (b, 0, 0)),
        ],
        out_shape=[
            jax.ShapeDtypeStruct((nb, seq, D_MODEL), F32),
            jax.ShapeDtypeStruct((nb, H_A, DK, DV), F32),
            jax.ShapeDtypeStruct((nb, CONV_A - 1, QKV_DIM), F32),
        ],
        scratch_shapes=[
            pltpu.VMEM((tb + SUBLANES, QKV_DIM), F32),
            pltpu.VMEM((H_A, DK, DV), F32),
        ],
        compiler_params=pltpu.CompilerParams(
            dimension_semantics=("arbitrary", "arbitrary"), vmem_limit_bytes=VMEM_LIMIT),
        name="prompt_mixer",
    )(x, mod, norm1, w_in, convw, alog_row, dtb_row, gdn, sgun, wsgu, bsgu_t, w_out)


def _ffn_body(x, mod_row, norm2, w_up_ref, upc_fn, w_down_ref):
    sh2 = mod_row[:, 3 * D_MODEL:4 * D_MODEL]
    sc2 = mod_row[:, 4 * D_MODEL:5 * D_MODEL]
    g2 = mod_row[:, 5 * D_MODEL:6 * D_MODEL]
    h2 = _rms(x, norm2) * (1.0 + sc2) + sh2
    up = _dot(h2, w_up_ref[...])
    upc = upc_fn(up)
    gated = jax.nn.silu(upc[:, 0:D_FF]) * upc[:, D_FF:2 * D_FF]
    return x + g2 * _dot(gated, w_down_ref[...])


def _ffn_kernel(x_ref, mod_ref, norm2_ref, w_up_ref, convw_ref, convb_ref, w_down_ref, fnorm_ref,
                xo_ref, ffnbuf_ref, xp_scr, *, tb, final):
    t = pl.program_id(1)

    @pl.when(t == 0)
    def _():
        xp_scr[0:SUBLANES, :] = jnp.zeros((SUBLANES, 2 * D_FF), F32)

    def conv(up):
        xp_scr[SUBLANES:SUBLANES + tb, :] = up
        acc = up * convw_ref[CONV_F - 1:CONV_F, :] + convb_ref[...]
        for j in range(CONV_F - 1):
            off = SUBLANES - (CONV_F - 1) + j
            acc = acc + xp_scr[off:off + tb, :] * convw_ref[j:j + 1, :]
        ffnbuf_ref[0] = xp_scr[tb + SUBLANES - (CONV_F - 1):tb + SUBLANES, :]
        xp_scr[0:SUBLANES, :] = xp_scr[tb:tb + SUBLANES, :]
        return acc

    x_new = _ffn_body(x_ref[0], mod_ref[0], norm2_ref[...], w_up_ref, conv, w_down_ref)
    if final:
        x_new = _rms(x_new, fnorm_ref[...])
    xo_ref[0] = x_new


def _prompt_ffn(x, mod, norm2, w_up, convw, convb, w_down, fnorm, final):
    nb, seq, _ = x.shape
    tb = TB_FFN
    kern = functools.partial(_ffn_kernel, tb=tb, final=final)
    return pl.pallas_call(
        kern,
        grid=(nb, seq // tb),
        in_specs=[
            pl.BlockSpec((1, tb, D_MODEL), lambda b, t: (b, t, 0)),
            pl.BlockSpec((1, 1, N_MOD * D_MODEL), lambda b, t: (b, 0, 0)),
            _const_spec((1, D_MODEL)),
            _const_spec((D_MODEL, 2 * D_FF)),
            _const_spec((CONV_F, 2 * D_FF)),
            _const_spec((1, 2 * D_FF)),
            _const_spec((D_FF, D_MODEL)),
            _const_spec((1, D_MODEL)),
        ],
        out_specs=[
            pl.BlockSpec((1, tb, D_MODEL), lambda b, t: (b, t, 0)),
            pl.BlockSpec((1, CONV_F - 1, 2 * D_FF), lambda b, t: (b, 0, 0)),
        ],
        out_shape=[
            jax.ShapeDtypeStruct((nb, seq, D_MODEL), F32),
            jax.ShapeDtypeStruct((nb, CONV_F - 1, 2 * D_FF), F32),
        ],
        scratch_shapes=[pltpu.VMEM((tb + SUBLANES, 2 * D_FF), F32)],
        compiler_params=pltpu.CompilerParams(
            dimension_semantics=("arbitrary", "arbitrary"), vmem_limit_bytes=VMEM_LIMIT),
        name="prompt_ffn",
    )(x, mod, norm2, w_up, convw, convb, w_down, fnorm)


def _bcast_heads(cols):
    return jnp.concatenate([jnp.broadcast_to(c, (c.shape[0], LANES)) for c in cols], axis=1)


def _sample_in_kernel(x_ref, mod_ref, norm1_ref, w_in_ref, cstate_ref, convw_ref, alog_ref, dtb_ref,
                      sgun_ref, wd_ref, b0_ref,
                      ke_ref, qe_ref, kn_ref, v_ref, beta_ref, eg_ref, qk_ref, z_ref, ob_ref, vv_ref,
                      cnew_ref):
    x = x_ref[...]
    sh1 = mod_ref[:, 0:D_MODEL]
    sc1 = mod_ref[:, D_MODEL:2 * D_MODEL]
    h = _rms(x, norm1_ref[...]) * (1.0 + sc1) + sh1
    proj = _dot(h, w_in_ref[...])
    qkv = proj[:, 0:QKV_DIM]
    acc = qkv * convw_ref[CONV_A - 1:CONV_A, :]
    for j in range(CONV_A - 1):
        acc = acc + cstate_ref[:, j * QKV_DIM:(j + 1) * QKV_DIM] * convw_ref[j:j + 1, :]
    for j in range(CONV_A - 2):
        cnew_ref[:, j * QKV_DIM:(j + 1) * QKV_DIM] = cstate_ref[:, (j + 1) * QKV_DIM:(j + 2) * QKV_DIM]
    cnew_ref[:, (CONV_A - 2) * QKV_DIM:(CONV_A - 1) * QKV_DIM] = qkv
    qkvc = jax.nn.silu(acc)

    sm = proj[:, P_SM:P_SM + LANES]
    beta_all = jax.nn.sigmoid(sm)
    eg_all = jnp.exp(_decay_logits(sm, alog_ref[...], dtb_ref[...]))
    beta_cols, eg_cols, qk_cols = [], [], []
    for hh in range(H_A):
        q_h = _l2n(qkvc[:, hh * DK:(hh + 1) * DK]) * (DK ** -0.5)
        k_h = _l2n(qkvc[:, QK_DIM + hh * DK:QK_DIM + (hh + 1) * DK])
        eg_c = _lane_col(eg_all, H_A + hh)
        sl = slice(hh * DK, (hh + 1) * DK)
        ke_ref[:, sl] = k_h * eg_c
        qe_ref[:, sl] = q_h * eg_c
        kn_ref[:, sl] = k_h
        beta_cols.append(_lane_col(beta_all, hh))
        eg_cols.append(eg_c)
        qk_cols.append(jnp.sum(q_h * k_h, axis=1, keepdims=True))
    v_ref[...] = qkvc[:, 2 * QK_DIM:2 * QK_DIM + V_DIM]
    beta_ref[...] = _bcast_heads(beta_cols)
    eg_ref[...] = _bcast_heads(eg_cols)
    qk_ref[...] = _bcast_heads(qk_cols)
    z_ref[...] = proj[:, P_Z:P_Z + V_DIM]

    vv_parts = []
    for hh in range(H_B):
        vv_parts.append(_rms(jax.nn.gelu(proj[:, P_VB + hh * D_B:P_VB + (hh + 1) * D_B]),
                             sgun_ref[:, hh * D_B:(hh + 1) * D_B]))
    vv = jnp.concatenate(vv_parts, axis=1)
    vv_ref[...] = vv
    ob_ref[...] = jax.nn.gelu(proj[:, P_U:P_U + SGU_DIM]) * (wd_ref[...] * vv + b0_ref[...])


def _sample_in(x, mod, norm1, w_in, cstate, convw, alog_row, dtb_row, sgun, wd_row, b0_row):
    n = x.shape[0]
    wide = jax.ShapeDtypeStruct((n, V_DIM), F32)
    ins = (x, mod, norm1, w_in, cstate, convw, alog_row, dtb_row, sgun, wd_row, b0_row)
    outs = [wide] * 10 + [jax.ShapeDtypeStruct((n, (CONV_A - 1) * QKV_DIM), F32)]
    return pl.pallas_call(
        _sample_in_kernel,
        grid=(1,),
        in_specs=[_const_spec(a.shape) for a in ins],
        out_specs=[pl.BlockSpec(o.shape, lambda i: (0, 0)) for o in outs],
        out_shape=outs,
        compiler_params=pltpu.CompilerParams(
            dimension_semantics=("arbitrary",), vmem_limit_bytes=VMEM_LIMIT),
        name="sample_in",
    )(*ins)


def _sample_state_kernel(s_ref, ke_ref, qe_ref, kn_ref, v_ref, beta_ref, eg_ref, qk_ref,
                         s_out_ref, o_ref):
    bb = BB_STATE
    row = lax.broadcasted_iota(jnp.int32, (bb, DK), 0)
    for hh in range(H_A):
        sl = slice(hh * DK, (hh + 1) * DK)
        lhs = jnp.concatenate([ke_ref[:, sl], qe_ref[:, sl]], axis=0)
        pred_k = jnp.zeros((bb, DV), F32)
        pred_q = jnp.zeros((bb, DV), F32)
        for i in range(bb):
            r = jnp.dot(lhs, s_ref[i, hh], preferred_element_type=F32)
            pred_k = pred_k + jnp.where(row == i, r[0:bb], 0.0)
            pred_q = pred_q + jnp.where(row == i, r[bb:2 * bb], 0.0)
        v_new = beta_ref[:, sl] * (v_ref[:, sl] - pred_k)
        o_ref[:, sl] = pred_q + qk_ref[:, sl] * v_new
        kn = kn_ref[:, sl]
        eg = eg_ref[:, sl]
        for i in range(bb):
            upd = lax.dot_general(jnp.where(row == i, kn, 0.0), v_new, (((0,), (0,)), ((), ())),
                                  preferred_element_type=F32)
            s_out_ref[i, hh] = s_ref[i, hh] * eg[i:i + 1, :] + upd


def _sample_state(s, ke, qe, kn, v, beta, eg, qk):
    n = s.shape[0]
    bb = BB_STATE
    row_spec = pl.BlockSpec((bb, V_DIM), lambda i: (i, 0))
    s_spec = pl.BlockSpec((bb, H_A, DK, DV), lambda i: (i, 0, 0, 0))
    return pl.pallas_call(
        _sample_state_kernel,
        grid=(n // bb,),
        in_specs=[s_spec] + [row_spec] * 7,
        out_specs=[s_spec, row_spec],
        out_shape=[jax.ShapeDtypeStruct(s.shape, F32), jax.ShapeDtypeStruct((n, V_DIM), F32)],
        compiler_params=pltpu.CompilerParams(
            dimension_semantics=("arbitrary",), vmem_limit_bytes=VMEM_LIMIT),
        name="sample_state",
    )(s, ke, qe, kn, v, beta, eg, qk)


def _sample_out_kernel(x_ref, mod_ref, o_ref, z_ref, ob_ref, gdn_ref, w_out_ref, norm2_ref, w_up_ref,
                       fstate_ref, convw_ref, convb_ref, w_down_ref, fnorm_ref,
                       xo_ref, fnew_ref, *, final):
    x = x_ref[...]
    g1 = mod_ref[:, 2 * D_MODEL:3 * D_MODEL]
    parts = []
    for hh in range(H_A):
        sl = slice(hh * DV, (hh + 1) * DV)
        parts.append(_rms(o_ref[:, sl], gdn_ref[...]) * jax.nn.silu(z_ref[:, sl]))
    parts.append(ob_ref[...])
    mix = jnp.concatenate(parts, axis=1)
    x = x + g1 * _dot(mix, w_out_ref[...])

    def conv(up):
        acc = up * convw_ref[CONV_F - 1:CONV_F, :] + convb_ref[...]
        for j in range(CONV_F - 1):
            acc = acc + fstate_ref[:, j * 2 * D_FF:(j + 1) * 2 * D_FF] * convw_ref[j:j + 1, :]
        for j in range(CONV_F - 2):
            fnew_ref[:, j * 2 * D_FF:(j + 1) * 2 * D_FF] = fstate_ref[:, (j + 1) * 2 * D_FF:(j + 2) * 2 * D_FF]
        fnew_ref[:, (CONV_F - 2) * 2 * D_FF:(CONV_F - 1) * 2 * D_FF] = up
        return acc

    x_new = _ffn_body(x, mod_ref[...], norm2_ref[...], w_up_ref, conv, w_down_ref)
    if final:
        x_new = _rms(x_new, fnorm_ref[...])
    xo_ref[...] = x_new


def _sample_out(x, mod, o, z, ob, gdn, w_out, norm2, w_up, fstate, convw, convb, w_down, fnorm, final):
    n = x.shape[0]
    ins = (x, mod, o, z, ob, gdn, w_out, norm2, w_up, fstate, convw, convb, w_down, fnorm)
    outs = [jax.ShapeDtypeStruct((n, D_MODEL), F32),
            jax.ShapeDtypeStruct((n, (CONV_F - 1) * 2 * D_FF), F32)]
    return pl.pallas_call(
        functools.partial(_sample_out_kernel, final=final),
        grid=(1,),
        in_specs=[_const_spec(a.shape) for a in ins],
        out_specs=[pl.BlockSpec(o_.shape, lambda i: (0, 0)) for o_ in outs],
        out_shape=outs,
        compiler_params=pltpu.CompilerParams(
            dimension_semantics=("arbitrary",), vmem_limit_bytes=VMEM_LIMIT),
        name="sample_out",
    )(*ins)


def _lane_row(vals, offset):
    return jnp.zeros((1, LANES), F32).at[0, offset:offset + vals.shape[0]].set(vals)


def kernel(x_prompt, x_sample, state_delta, state_qkv_conv, state_ffn_conv, c_prompt, c_sample,
           w_mod, b_mod, norm1, w_in, conv_qkv, a_log, dt_bias, gdn_norm, sgu_norm, w_sgu, b_sgu,
           w_out, norm2, w_up, conv_ffn_w, conv_ffn_b, w_down, final_norm):
    nbp = x_prompt.shape[0]
    nbs = x_sample.shape[0]

    off_z = QKV_DIM
    off_beta = off_z + V_DIM
    off_u = off_beta + 2 * H_A
    off_vb = off_u + SGU_DIM
    w_in_p = jnp.concatenate(
        [w_in[:, :, 0:off_beta], w_in[:, :, off_u:off_vb + SGU_DIM], w_in[:, :, off_beta:off_u],
         jnp.zeros((DEPTH, D_MODEL, LANES - 2 * H_A), w_in.dtype)], axis=2).astype(BF16)
    w_out_b = w_out.astype(BF16)
    w_up_b = w_up.astype(BF16)
    w_down_b = w_down.astype(BF16)

    mod_all = _modulation(jnp.concatenate([c_prompt, c_sample], axis=0), w_mod, b_mod)
    mod_p = mod_all[:, :nbp].reshape(DEPTH, nbp, 1, N_MOD * D_MODEL)
    mod_s = mod_all[:, nbp:]
    fnorm = final_norm.reshape(1, D_MODEL)

    xp = x_prompt
    xs = x_sample.reshape(nbs, D_MODEL)
    cstate = state_qkv_conv.reshape(DEPTH, nbs, (CONV_A - 1) * QKV_DIM)
    fstate = state_ffn_conv.reshape(DEPTH, nbs, (CONV_F - 1) * 2 * D_FF)
    delta_p, delta_s, qkv_p, qkv_s, ffn_p, ffn_s, vv_s = [], [], [], [], [], [], []
    for l in range(DEPTH):
        final = l == DEPTH - 1
        n1 = norm1[l].reshape(1, D_MODEL)
        n2 = norm2[l].reshape(1, D_MODEL)
        alog_row = _lane_row(a_log[l], H_A)
        dtb_row = _lane_row(dt_bias[l], H_A)
        gdn = gdn_norm[l].reshape(1, DV)
        sgun = sgu_norm[l].reshape(1, SGU_DIM)
        convb = conv_ffn_b[l].reshape(1, 2 * D_FF)
        bsgu_t = jnp.zeros((SGU_CHUNK, LANES), F32).at[:, 0:H_B].set(b_sgu[l].T)

        xp, s_p, qb_p = _prompt_mixer(xp, mod_p[l], n1, w_in_p[l], conv_qkv[l], alog_row, dtb_row,
                                      gdn, sgun, w_sgu[l], bsgu_t, w_out_b[l])
        xp, fb_p = _prompt_ffn(xp, mod_p[l], n2, w_up_b[l], conv_ffn_w[l], convb, w_down_b[l],
                               fnorm, final)
        delta_p.append(s_p)
        qkv_p.append(qb_p)
        ffn_p.append(fb_p)

        wd_row = jnp.repeat(w_sgu[l][:, 0, 0], D_B).reshape(1, SGU_DIM)
        b0_row = jnp.repeat(b_sgu[l][:, 0], D_B).reshape(1, SGU_DIM)
        (ke, qe, kn, v, beta, eg, qk, z, ob, vv, c_new) = _sample_in(
            xs, mod_s[l], n1, w_in_p[l], cstate[l], conv_qkv[l], alog_row, dtb_row, sgun,
            wd_row, b0_row)
        s_s, o = _sample_state(state_delta[l], ke, qe, kn, v, beta, eg, qk)
        xs, f_new = _sample_out(xs, mod_s[l], o, z, ob, gdn, w_out_b[l], n2, w_up_b[l], fstate[l],
                                conv_ffn_w[l], convb, w_down_b[l], fnorm, final)
        delta_s.append(s_s)
        qkv_s.append(c_new.reshape(nbs, CONV_A - 1, QKV_DIM))
        ffn_s.append(f_new.reshape(nbs, CONV_F - 1, 2 * D_FF))
        vv_s.append(vv.reshape(nbs, 1, H_B, D_B))

    return (xp, xs.reshape(nbs, 1, D_MODEL), jnp.stack(delta_p), jnp.stack(delta_s),
            jnp.stack(qkv_p), jnp.stack(qkv_s), jnp.stack(ffn_p), jnp.stack(ffn_s),
            jnp.stack(vv_s))
```

```python
import functools
import math

import jax
import jax.numpy as jnp
from jax import lax
from jax.experimental import pallas as pl
from jax.experimental.pallas import tpu as pltpu

F32 = jnp.float32
BF16 = jnp.bfloat16

D_MODEL = 1024
DEPTH = 4
H_A = 4
DK = 128
DV = 128
QK_DIM = H_A * DK
V_DIM = H_A * DV
QKV_DIM = 2 * QK_DIM + V_DIM
CONV_A = 4
H_B = 4
D_B = 128
SGU_DIM = H_B * D_B
SGU_CHUNK = 128
D_FF = 2816
CONV_F = 3
N_MOD = 6
EPS = 1e-6

P_Z = QKV_DIM
P_U = P_Z + V_DIM
P_VB = P_U + SGU_DIM
P_SM = P_VB + SGU_DIM
P_DIM = P_SM + 128

LANES = 128
SUBLANES = 8
SUPER = 128
DELTA_C = 64
TB_MIX = 256
TB_FFN = 256
BB_STATE = 8
VMEM_LIMIT = 56 * 1024 * 1024


def _dot(a, b):
    return jnp.dot(a.astype(BF16), b.astype(BF16), preferred_element_type=F32)


def _dot_nt(a, b):
    return lax.dot_general(a.astype(BF16), b.astype(BF16), (((1,), (1,)), ((), ())),
                           preferred_element_type=F32)


def _dot_tn(a, b):
    return lax.dot_general(a.astype(BF16), b.astype(BF16), (((0,), (0,)), ((), ())),
                           preferred_element_type=F32)


def _rms(x, w):
    return x * lax.rsqrt(jnp.mean(x * x, axis=-1, keepdims=True) + EPS) * w


def _l2n(x):
    return x * lax.rsqrt(jnp.sum(x * x, axis=-1, keepdims=True) + EPS)


def _lane_col(x, lane):
    ids = lax.broadcasted_iota(jnp.int32, x.shape, 1)
    return jnp.sum(jnp.where(ids == lane, x, 0.0), axis=1, keepdims=True)


def _decay_logits(sm, alog_row, dtb_row):
    return -jnp.exp(alog_row) * jax.nn.softplus(sm + dtb_row)


def _const_spec(shape):
    nd = len(shape)
    return pl.BlockSpec(shape, lambda *_: (0,) * nd, pipeline_mode=pl.Buffered(1))


def _mod_kernel(c_ref, w_ref, b_ref, o_ref):
    c_act = jax.nn.silu(c_ref[...])
    o_ref[0] = _dot(c_act, w_ref[0]) + b_ref[0]


def _modulation(c_all, w_mod, b_mod):
    n = c_all.shape[0]
    tn = 1536
    return pl.pallas_call(
        _mod_kernel,
        grid=(DEPTH, (N_MOD * D_MODEL) // tn),
        in_specs=[
            pl.BlockSpec((n, D_MODEL), lambda l, j: (0, 0)),
            pl.BlockSpec((1, D_MODEL, tn), lambda l, j: (l, 0, j)),
            pl.BlockSpec((1, 1, tn), lambda l, j: (l, 0, j)),
        ],
        out_specs=pl.BlockSpec((1, n, tn), lambda l, j: (l, 0, j)),
        out_shape=jax.ShapeDtypeStruct((DEPTH, n, N_MOD * D_MODEL), F32),
        compiler_params=pltpu.CompilerParams(
            dimension_semantics=("arbitrary", "arbitrary"), vmem_limit_bytes=VMEM_LIMIT),
        name="modulation",
    )(c_all, w_mod, b_mod.reshape(DEPTH, 1, N_MOD * D_MODEL))


def _mixer_kernel(x_ref, mod_ref, norm1_ref, w_in_ref, convw_ref, alog_ref, dtb_ref,
                  gdn_ref, sgun_ref, wsgu_ref, bsgu_ref, w_out_ref,
                  xo_ref, s_out_ref, qkvbuf_ref,
                  xp_scr, s_scr, *, tb):
    t = pl.program_id(1)
    nt = pl.num_programs(1)

    @pl.when(t == 0)
    def _():
        xp_scr[0:SUBLANES, :] = jnp.zeros((SUBLANES, QKV_DIM), F32)
        s_scr[...] = jnp.zeros(s_scr.shape, F32)

    x = x_ref[0]
    sh1 = mod_ref[0, :, 0:D_MODEL]
    sc1 = mod_ref[0, :, D_MODEL:2 * D_MODEL]
    g1 = mod_ref[0, :, 2 * D_MODEL:3 * D_MODEL]
    h = _rms(x, norm1_ref[...]) * (1.0 + sc1) + sh1
    proj = _dot(h, w_in_ref[...])

    qkv = proj[:, 0:QKV_DIM]
    xp_scr[SUBLANES:SUBLANES + tb, :] = qkv
    acc = qkv * convw_ref[3:4, :]
    for j in range(CONV_A - 1):
        off = SUBLANES - (CONV_A - 1) + j
        acc = acc + xp_scr[off:off + tb, :] * convw_ref[j:j + 1, :]
    qkvbuf_ref[0] = xp_scr[tb + SUBLANES - (CONV_A - 1):tb + SUBLANES, :]
    xp_scr[0:SUBLANES, :] = xp_scr[tb:tb + SUBLANES, :]
    qkvc = jax.nn.silu(acc)

    sm = proj[:, P_SM:P_SM + LANES]
    beta_all = jax.nn.sigmoid(sm)
    g_all = _decay_logits(sm, alog_ref[...], dtb_ref[...])

    ri = lax.broadcasted_iota(jnp.int32, (tb, tb), 0)
    ci = lax.broadcasted_iota(jnp.int32, (tb, tb), 1)
    cum_mat = jnp.where((ri // DELTA_C == ci // DELTA_C) & (ri >= ci), 1.0, 0.0).astype(BF16)
    g_hi = g_all.astype(BF16)
    r1 = g_all - g_hi.astype(F32)
    g_mid = r1.astype(BF16)
    g_lo = (r1 - g_mid.astype(F32)).astype(BF16)
    gc_all = (jnp.dot(cum_mat, g_hi, preferred_element_type=F32)
              + jnp.dot(cum_mat, g_mid, preferred_element_type=F32)
              + jnp.dot(cum_mat, g_lo, preferred_element_type=F32))
    gc_t = gc_all.T

    rs = lax.broadcasted_iota(jnp.int32, (SUPER, SUPER), 0)
    cs = lax.broadcasted_iota(jnp.int32, (SUPER, SUPER), 1)
    same_chunk = (rs // DELTA_C) == (cs // DELTA_C)
    tril = same_chunk & (rs >= cs)
    strict = same_chunk & (rs > cs)
    eye = jnp.where(rs == cs, 1.0, 0.0)
    pair_mask = (rs // 2 == cs // 2) & (rs > cs)
    quad_masks = []
    half = 2
    while half < DELTA_C:
        quad_masks.append((rs // (2 * half) == cs // (2 * half))
                          & (rs % (2 * half) >= half) & (cs % (2 * half) < half))
        half *= 2
    chunks_per_super = SUPER // DELTA_C

    row_chunk = lax.broadcasted_iota(jnp.int32, (SUPER, 1), 0) // DELTA_C
    n_super = tb // SUPER

    pairs = [(hh, s) for s in range(n_super) for hh in range(H_A)]
    head_in = []
    for hh in range(H_A):
        head_in.append((
            _l2n(qkvc[:, hh * DK:(hh + 1) * DK]) * (DK ** -0.5),
            _l2n(qkvc[:, QK_DIM + hh * DK:QK_DIM + (hh + 1) * DK]),
            qkvc[:, 2 * QK_DIM + hh * DV:2 * QK_DIM + (hh + 1) * DV],
            _lane_col(beta_all, hh),
            _lane_col(gc_all, H_A + hh)))

    low_p, qk_p, rhs_p, qg_p, kdec_p, glast_p = {}, {}, {}, {}, {}, {}
    for p in pairs:
        hh, s = p
        q_h, k_h, v_h, beta_c, gc_c = head_in[hh]
        r0 = s * SUPER
        qs = q_h[r0:r0 + SUPER]
        ks = k_h[r0:r0 + SUPER]
        bs = beta_c[r0:r0 + SUPER]
        gcol = gc_c[r0:r0 + SUPER]
        grow = gc_t[H_A + hh:H_A + hh + 1, r0:r0 + SUPER]
        decay = jnp.exp(jnp.where(tril, gcol - grow, -jnp.inf))
        kb = ks * bs
        eg = jnp.exp(gcol)
        low_p[p] = jnp.where(strict, _dot_nt(kb, ks) * decay, 0.0)
        qk_p[p] = jnp.where(tril, _dot_nt(qs, ks) * decay, 0.0)
        rhs_p[p] = jnp.concatenate([v_h[r0:r0 + SUPER] * bs, kb * eg], axis=1)
        qg_p[p] = qs * eg
        g_last = [grow[:, (c + 1) * DELTA_C - 1:(c + 1) * DELTA_C] for c in range(chunks_per_super)]
        glast_col = g_last[-1]
        for c in range(chunks_per_super - 2, -1, -1):
            glast_col = jnp.where(row_chunk == c, g_last[c], glast_col)
        kdec_p[p] = ks * jnp.exp(glast_col - gcol)
        glast_p[p] = g_last

    inv_p = {p: eye - jnp.where(pair_mask, low_p[p], 0.0) for p in pairs}
    for qm in quad_masks:
        t_p = {p: _dot(inv_p[p], jnp.where(qm, low_p[p], 0.0)) for p in pairs}
        inv_p = {p: inv_p[p] - _dot(t_p[p], inv_p[p]) for p in pairs}
    sol_p = {p: _dot(inv_p[p], rhs_p[p]) for p in pairs}
    qs_p = {p: _dot(qk_p[p], sol_p[p]) for p in pairs}
    bm_p = {(p, c): _dot_tn(kdec_p[p][c * DELTA_C:(c + 1) * DELTA_C],
                            sol_p[p][c * DELTA_C:(c + 1) * DELTA_C])
            for p in pairs for c in range(chunks_per_super)}

    state = [s_scr[hh] for hh in range(H_A)]
    o_rows = [[] for _ in range(H_A)]
    for s in range(n_super):
        for c in range(chunks_per_super):
            c0 = c * DELTA_C
            for hh in range(H_A):
                p = (hh, s)
                s_b = state[hh].astype(BF16)
                q_eff = qg_p[p][c0:c0 + DELTA_C] - qs_p[p][c0:c0 + DELTA_C, DV:DV + DK]
                o_rows[hh].append(_dot(q_eff, s_b) + qs_p[p][c0:c0 + DELTA_C, 0:DV])
                bm = bm_p[(p, c)]
                state[hh] = (state[hh] * jnp.exp(glast_p[p][c]) - _dot(bm[:, DV:DV + DK], s_b)
                             + bm[:, 0:DV])
    o_heads = []
    for hh in range(H_A):
        s_scr[hh] = state[hh]
        o_h = jnp.concatenate(o_rows[hh], axis=0)
        z_h = proj[:, P_Z + hh * DV:P_Z + (hh + 1) * DV]
        o_heads.append(_rms(o_h, gdn_ref[...]) * jax.nn.silu(z_h))

    rc = lax.broadcasted_iota(jnp.int32, (SGU_CHUNK, SGU_CHUNK), 0)
    cc = lax.broadcasted_iota(jnp.int32, (SGU_CHUNK, SGU_CHUNK), 1)
    for hh in range(H_B):
        u_h = jax.nn.gelu(proj[:, P_U + hh * D_B:P_U + (hh + 1) * D_B])
        v_h = _rms(jax.nn.gelu(proj[:, P_VB + hh * D_B:P_VB + (hh + 1) * D_B]),
                   sgun_ref[:, hh * D_B:(hh + 1) * D_B])
        w_h = jnp.where(rc >= cc, wsgu_ref[hh], 0.0)
        b_col = _lane_col(bsgu_ref[...], hh)
        z_rows = []
        for s in range(tb // SGU_CHUNK):
            r0 = s * SGU_CHUNK
            z_rows.append(_dot(w_h, v_h[r0:r0 + SGU_CHUNK]) + b_col)
        o_heads.append(u_h * jnp.concatenate(z_rows, axis=0))

    mix = jnp.concatenate(o_heads, axis=1)
    xo_ref[0] = x + g1 * _dot(mix, w_out_ref[...])

    @pl.when(t == nt - 1)
    def _():
        s_out_ref[0] = s_scr[...]


def _prompt_mixer(x, mod, norm1, w_in, convw, alog_row, dtb_row, gdn, sgun, wsgu, bsgu_t, w_out):
    nb, seq, _ = x.shape
    tb = TB_MIX
    kern = functools.partial(_mixer_kernel, tb=tb)
    return pl.pallas_call(
        kern,
        grid=(nb, seq // tb),
        in_specs=[
            pl.BlockSpec((1, tb, D_MODEL), lambda b, t: (b, t, 0)),
            pl.BlockSpec((1, 1, N_MOD * D_MODEL), lambda b, t: (b, 0, 0)),
            _const_spec((1, D_MODEL)),
            _const_spec((D_MODEL, P_DIM)),
            _const_spec((CONV_A, QKV_DIM)),
            _const_spec((1, LANES)),
            _const_spec((1, LANES)),
            _const_spec((1, DV)),
            _const_spec((1, SGU_DIM)),
            _const_spec((H_B, SGU_CHUNK, SGU_CHUNK)),
            _const_spec((SGU_CHUNK, LANES)),
            _const_spec((D_MODEL, D_MODEL)),
        ],
        out_specs=[
            pl.BlockSpec((1, tb, D_MODEL), lambda b, t: (b, t, 0)),
            pl.BlockSpec((1, H_A, DK, DV), lambda b, t: (b, 0, 0, 0)),
            pl.BlockSpec((1, CONV_A - 1, QKV_DIM), lambda b, t: (b, 0, 0)),
        ],
        out_shape=[
            jax.ShapeDtypeStruct((nb, seq, D_MODEL), F32),
            jax.ShapeDtypeStruct((nb, H_A, DK, DV), F32),
            jax.ShapeDtypeStruct((nb, CONV_A - 1, QKV_DIM), F32),
        ],
        scratch_shapes=[
            pltpu.VMEM((tb + SUBLANES, QKV_DIM), F32),
            pltpu.VMEM((H_A, DK, DV), F32),
        ],
        compiler_params=pltpu.CompilerParams(
            dimension_semantics=("arbitrary", "arbitrary"), vmem_limit_bytes=VMEM_LIMIT),
        name="prompt_mixer",
    )(x, mod, norm1, w_in, convw, alog_row, dtb_row, gdn, sgun, wsgu, bsgu_t, w_out)


def _ffn_body(x, mod_row, norm2, w_up_ref, upc_fn, w_down_ref):
    sh2 = mod_row[:, 3 * D_MODEL:4 * D_MODEL]
    sc2 = mod_row[:, 4 * D_MODEL:5 * D_MODEL]
    g2 = mod_row[:, 5 * D_MODEL:6 * D_MODEL]
    h2 = _rms(x, norm2) * (1.0 + sc2) + sh2
    up = _dot(h2, w_up_ref[...])
    upc = upc_fn(up)
    gated = jax.nn.silu(upc[:, 0:D_FF]) * upc[:, D_FF:2 * D_FF]
    return x + g2 * _dot(gated, w_down_ref[...])


def _ffn_kernel(x_ref, mod_ref, norm2_ref, w_up_ref, convw_ref, convb_ref, w_down_ref, fnorm_ref,
                xo_ref, ffnbuf_ref, xp_scr, *, tb, final):
    t = pl.program_id(1)

    @pl.when(t == 0)
    def _():
        xp_scr[0:SUBLANES, :] = jnp.zeros((SUBLANES, 2 * D_FF), F32)

    def conv(up):
        xp_scr[SUBLANES:SUBLANES + tb, :] = up
        acc = up * convw_ref[CONV_F - 1:CONV_F, :] + convb_ref[...]
        for j in range(CONV_F - 1):
            off = SUBLANES - (CONV_F - 1) + j
            acc = acc + xp_scr[off:off + tb, :] * convw_ref[j:j + 1, :]
        ffnbuf_ref[0] = xp_scr[tb + SUBLANES - (CONV_F - 1):tb + SUBLANES, :]
        xp_scr[0:SUBLANES, :] = xp_scr[tb:tb + SUBLANES, :]
        return acc

    x_new = _ffn_body(x_ref[0], mod_ref[0], norm2_ref[...], w_up_ref, conv, w_down_ref)
    if final:
        x_new = _rms(x_new, fnorm_ref[...])
    xo_ref[0] = x_new


def _prompt_ffn(x, mod, norm2, w_up, convw, convb, w_down, fnorm, final):
    nb, seq, _ = x.shape
    tb = TB_FFN
    kern = functools.partial(_ffn_kernel, tb=tb, final=final)
    return pl.pallas_call(
        kern,
        grid=(nb, seq // tb),
        in_specs=[
            pl.BlockSpec((1, tb, D_MODEL), lambda b, t: (b, t, 0)),
            pl.BlockSpec((1, 1, N_MOD * D_MODEL), lambda b, t: (b, 0, 0)),
            _const_spec((1, D_MODEL)),
            _const_spec((D_MODEL, 2 * D_FF)),
            _const_spec((CONV_F, 2 * D_FF)),
            _const_spec((1, 2 * D_FF)),
            _const_spec((D_FF, D_MODEL)),
            _const_spec((1, D_MODEL)),
        ],
        out_specs=[
            pl.BlockSpec((1, tb, D_MODEL), lambda b, t: (b, t, 0)),
            pl.BlockSpec((1, CONV_F - 1, 2 * D_FF), lambda b, t: (b, 0, 0)),
        ],
        out_shape=[
            jax.ShapeDtypeStruct((nb, seq, D_MODEL), F32),
            jax.ShapeDtypeStruct((nb, CONV_F - 1, 2 * D_FF), F32),
        ],
        scratch_shapes=[pltpu.VMEM((tb + SUBLANES, 2 * D_FF), F32)],
        compiler_params=pltpu.CompilerParams(
            dimension_semantics=("arbitrary", "arbitrary"), vmem_limit_bytes=VMEM_LIMIT),
        name="prompt_ffn",
    )(x, mod, norm2, w_up, convw, convb, w_down, fnorm)


def _bcast_heads(cols):
    return jnp.concatenate([jnp.broadcast_to(c, (c.shape[0], LANES)) for c in cols], axis=1)


def _sample_in_kernel(x_ref, mod_ref, norm1_ref, w_in_ref, cstate_ref, convw_ref, alog_ref, dtb_ref,
                      sgun_ref, wd_ref, b0_ref,
                      ke_ref, qe_ref, kn_ref, v_ref, beta_ref, eg_ref, qk_ref, z_ref, ob_ref, vv_ref,
                      cnew_ref):
    x = x_ref[...]
    sh1 = mod_ref[:, 0:D_MODEL]
    sc1 = mod_ref[:, D_MODEL:2 * D_MODEL]
    h = _rms(x, norm1_ref[...]) * (1.0 + sc1) + sh1
    proj = _dot(h, w_in_ref[...])
    qkv = proj[:, 0:QKV_DIM]
    acc = qkv * convw_ref[CONV_A - 1:CONV_A, :]
    for j in range(CONV_A - 1):
        acc = acc + cstate_ref[:, j * QKV_DIM:(j + 1) * QKV_DIM] * convw_ref[j:j + 1, :]
    for j in range(CONV_A - 2):
        cnew_ref[:, j * QKV_DIM:(j + 1) * QKV_DIM] = cstate_ref[:, (j + 1) * QKV_DIM:(j + 2) * QKV_DIM]
    cnew_ref[:, (CONV_A - 2) * QKV_DIM:(CONV_A - 1) * QKV_DIM] = qkv
    qkvc = jax.nn.silu(acc)

    sm = proj[:, P_SM:P_SM + LANES]
    beta_all = jax.nn.sigmoid(sm)
    eg_all = jnp.exp(_decay_logits(sm, alog_ref[...], dtb_ref[...]))
    beta_cols, eg_cols, qk_cols = [], [], []
    for hh in range(H_A):
        q_h = _l2n(qkvc[:, hh * DK:(hh + 1) * DK]) * (DK ** -0.5)
        k_h = _l2n(qkvc[:, QK_DIM + hh * DK:QK_DIM + (hh + 1) * DK])
        eg_c = _lane_col(eg_all, H_A + hh)
        sl = slice(hh * DK, (hh + 1) * DK)
        ke_ref[:, sl] = k_h * eg_c
        qe_ref[:, sl] = q_h * eg_c
        kn_ref[:, sl] = k_h
        beta_cols.append(_lane_col(beta_all, hh))
        eg_cols.append(eg_c)
        qk_cols.append(jnp.sum(q_h * k_h, axis=1, keepdims=True))
    v_ref[...] = qkvc[:, 2 * QK_DIM:2 * QK_DIM + V_DIM]
    beta_ref[...] = _bcast_heads(beta_cols)
    eg_ref[...] = _bcast_heads(eg_cols)
    qk_ref[...] = _bcast_heads(qk_cols)
    z_ref[...] = proj[:, P_Z:P_Z + V_DIM]

    vv_parts = []
    for hh in range(H_B):
        vv_parts.append(_rms(jax.nn.gelu(proj[:, P_VB + hh * D_B:P_VB + (hh + 1) * D_B]),
                             sgun_ref[:, hh * D_B:(hh + 1) * D_B]))
    vv = jnp.concatenate(vv_parts, axis=1)
    vv_ref[...] = vv
    ob_ref[...] = jax.nn.gelu(proj[:, P_U:P_U + SGU_DIM]) * (wd_ref[...] * vv + b0_ref[...])


def _sample_in(x, mod, norm1, w_in, cstate, convw, alog_row, dtb_row, sgun, wd_row, b0_row):
    n = x.shape[0]
    wide = jax.ShapeDtypeStruct((n, V_DIM), F32)
    ins = (x, mod, norm1, w_in, cstate, convw, alog_row, dtb_row, sgun, wd_row, b0_row)
    outs = [wide] * 10 + [jax.ShapeDtypeStruct((n, (CONV_A - 1) * QKV_DIM), F32)]
    return pl.pallas_call(
        _sample_in_kernel,
        grid=(1,),
        in_specs=[_const_spec(a.shape) for a in ins],
        out_specs=[pl.BlockSpec(o.shape, lambda i: (0, 0)) for o in outs],
        out_shape=outs,
        compiler_params=pltpu.CompilerParams(
            dimension_semantics=("arbitrary",), vmem_limit_bytes=VMEM_LIMIT),
        name="sample_in",
    )(*ins)


def _sample_state_kernel(s_ref, ke_ref, qe_ref, kn_ref, v_ref, beta_ref, eg_ref, qk_ref,
                         s_out_ref, o_ref):
    bb = BB_STATE
    row = lax.broadcasted_iota(jnp.int32, (bb, DK), 0)
    for hh in range(H_A):
        sl = slice(hh * DK, (hh + 1) * DK)
        lhs = jnp.concatenate([ke_ref[:, sl], qe_ref[:, sl]], axis=0)
        pred_k = jnp.zeros((bb, DV), F32)
        pred_q = jnp.zeros((bb, DV), F32)
        for i in range(bb):
            r = jnp.dot(lhs, s_ref[i, hh], preferred_element_type=F32)
            pred_k = pred_k + jnp.where(row == i, r[0:bb], 0.0)
            pred_q = pred_q + jnp.where(row == i, r[bb:2 * bb], 0.0)
        v_new = beta_ref[:, sl] * (v_ref[:, sl] - pred_k)
        o_ref[:, sl] = pred_q + qk_ref[:, sl] * v_new
        kn = kn_ref[:, sl]
        eg = eg_ref[:, sl]
        for i in range(bb):
            upd = lax.dot_general(jnp.where(row == i, kn, 0.0), v_new, (((0,), (0,)), ((), ())),
                                  preferred_element_type=F32)
            s_out_ref[i, hh] = s_ref[i, hh] * eg[i:i + 1, :] + upd


def _sample_state(s, ke, qe, kn, v, beta, eg, qk):
    n = s.shape[0]
    bb = BB_STATE
    row_spec = pl.BlockSpec((bb, V_DIM), lambda i: (i, 0))
    s_spec = pl.BlockSpec((bb, H_A, DK, DV), lambda i: (i, 0, 0, 0))
    return pl.pallas_call(
        _sample_state_kernel,
        grid=(n // bb,),
        in_specs=[s_spec] + [row_spec] * 7,
        out_specs=[s_spec, row_spec],
        out_shape=[jax.ShapeDtypeStruct(s.shape, F32), jax.ShapeDtypeStruct((n, V_DIM), F32)],
        compiler_params=pltpu.CompilerParams(
            dimension_semantics=("arbitrary",), vmem_limit_bytes=VMEM_LIMIT),
        name="sample_state",
    )(s, ke, qe, kn, v, beta, eg, qk)


def _sample_out_kernel(x_ref, mod_ref, o_ref, z_ref, ob_ref, gdn_ref, w_out_ref, norm2_ref, w_up_ref,
                       fstate_ref, convw_ref, convb_ref, w_down_ref, fnorm_ref,
                       xo_ref, fnew_ref, *, final):
    x = x_ref[...]
    g1 = mod_ref[:, 2 * D_MODEL:3 * D_MODEL]
    parts = []
    for hh in range(H_A):
        sl = slice(hh * DV, (hh + 1) * DV)
        parts.append(_rms(o_ref[:, sl], gdn_ref[...]) * jax.nn.silu(z_ref[:, sl]))
    parts.append(ob_ref[...])
    mix = jnp.concatenate(parts, axis=1)
    x = x + g1 * _dot(mix, w_out_ref[...])

    def conv(up):
        acc = up * convw_ref[CONV_F - 1:CONV_F, :] + convb_ref[...]
        for j in range(CONV_F - 1):
            acc = acc + fstate_ref[:, j * 2 * D_FF:(j + 1) * 2 * D_FF] * convw_ref[j:j + 1, :]
        for j in range(CONV_F - 2):
            fnew_ref[:, j * 2 * D_FF:(j + 1) * 2 * D_FF] = fstate_ref[:, (j + 1) * 2 * D_FF:(j + 2) * 2 * D_FF]
        fnew_ref[:, (CONV_F - 2) * 2 * D_FF:(CONV_F - 1) * 2 * D_FF] = up
        return acc

    x_new = _ffn_body(x, mod_ref[...], norm2_ref[...], w_up_ref, conv, w_down_ref)
    if final:
        x_new = _rms(x_new, fnorm_ref[...])
    xo_ref[...] = x_new


def _sample_out(x, mod, o, z, ob, gdn, w_out, norm2, w_up, fstate, convw, convb, w_down, fnorm, final):
    n = x.shape[0]
    ins = (x, mod, o, z, ob, gdn, w_out, norm2, w_up, fstate, convw, convb, w_down, fnorm)
    outs = [jax.ShapeDtypeStruct((n, D_MODEL), F32),
            jax.ShapeDtypeStruct((n, (CONV_F - 1) * 2 * D_FF), F32)]
    return pl.pallas_call(
        functools.partial(_sample_out_kernel, final=final),
        grid=(1,),
        in_specs=[_const_spec(a.shape) for a in ins],
        out_specs=[pl.BlockSpec(o_.shape, lambda i: (0, 0)) for o_ in outs],
        out_shape=outs,
        compiler_params=pltpu.CompilerParams(
            dimension_semantics=("arbitrary",), vmem_limit_bytes=VMEM_LIMIT),
        name="sample_out",
    )(*ins)


def _lane_row(vals, offset):
    return jnp.zeros((1, LANES), F32).at[0, offset:offset + vals.shape[0]].set(vals)


def kernel(x_prompt, x_sample, state_delta, state_qkv_conv, state_ffn_conv, c_prompt, c_sample,
           w_mod, b_mod, norm1, w_in, conv_qkv, a_log, dt_bias, gdn_norm, sgu_norm, w_sgu, b_sgu,
           w_out, norm2, w_up, conv_ffn_w, conv_ffn_b, w_down, final_norm):
    nbp = x_prompt.shape[0]
    nbs = x_sample.shape[0]

    off_z = QKV_DIM
    off_beta = off_z + V_DIM
    off_u = off_beta + 2 * H_A
    off_vb = off_u + SGU_DIM
    w_in_p = jnp.concatenate(
        [w_in[:, :, 0:off_beta], w_in[:, :, off_u:off_vb + SGU_DIM], w_in[:, :, off_beta:off_u],
         jnp.zeros((DEPTH, D_MODEL, LANES - 2 * H_A), w_in.dtype)], axis=2).astype(BF16)
    w_out_b = w_out.astype(BF16)
    w_up_b = w_up.astype(BF16)
    w_down_b = w_down.astype(BF16)

    mod_all = _modulation(jnp.concatenate([c_prompt, c_sample], axis=0), w_mod, b_mod)
    mod_p = mod_all[:, :nbp].reshape(DEPTH, nbp, 1, N_MOD * D_MODEL)
    mod_s = mod_all[:, nbp:]
    fnorm = final_norm.reshape(1, D_MODEL)

    xp = x_prompt
    xs = x_sample.reshape(nbs, D_MODEL)
    cstate = state_qkv_conv.reshape(DEPTH, nbs, (CONV_A - 1) * QKV_DIM)
    fstate = state_ffn_conv.reshape(DEPTH, nbs, (CONV_F - 1) * 2 * D_FF)
    delta_p, delta_s, qkv_p, qkv_s, ffn_p, ffn_s, vv_s = [], [], [], [], [], [], []
    for l in range(DEPTH):
        final = l == DEPTH - 1
        n1 = norm1[l].reshape(1, D_MODEL)
        n2 = norm2[l].reshape(1, D_MODEL)
        alog_row = _lane_row(a_log[l], H_A)
        dtb_row = _lane_row(dt_bias[l], H_A)
        gdn = gdn_norm[l].reshape(1, DV)
        sgun = sgu_norm[l].reshape(1, SGU_DIM)
        convb = conv_ffn_b[l].reshape(1, 2 * D_FF)
        bsgu_t = jnp.zeros((SGU_CHUNK, LANES), F32).at[:, 0:H_B].set(b_sgu[l].T)

        xp, s_p, qb_p = _prompt_mixer(xp, mod_p[l], n1, w_in_p[l], conv_qkv[l], alog_row, dtb_row,
                                      gdn, sgun, w_sgu[l], bsgu_t, w_out_b[l])
        xp, fb_p = _prompt_ffn(xp, mod_p[l], n2, w_up_b[l], conv_ffn_w[l], convb, w_down_b[l],
                               fnorm, final)
        delta_p.append(s_p)
        qkv_p.append(qb_p)
        ffn_p.append(fb_p)

        wd_row = jnp.repeat(w_sgu[l][:, 0, 0], D_B).reshape(1, SGU_DIM)
        b0_row = jnp.repeat(b_sgu[l][:, 0], D_B).reshape(1, SGU_DIM)
        (ke, qe, kn, v, beta, eg, qk, z, ob, vv, c_new) = _sample_in(
            xs, mod_s[l], n1, w_in_p[l], cstate[l], conv_qkv[l], alog_row, dtb_row, sgun,
            wd_row, b0_row)
        s_s, o = _sample_state(state_delta[l], ke, qe, kn, v, beta, eg, qk)
        xs, f_new = _sample_out(xs, mod_s[l], o, z, ob, gdn, w_out_b[l], n2, w_up_b[l], fstate[l],
                                conv_ffn_w[l], convb, w_down_b[l], fnorm, final)
        delta_s.append(s_s)
        qkv_s.append(c_new.reshape(nbs, CONV_A - 1, QKV_DIM))
        ffn_s.append(f_new.reshape(nbs, CONV_F - 1, 2 * D_FF))
        vv_s.append(vv.reshape(nbs, 1, H_B, D_B))

    return (xp, xs.reshape(nbs, 1, D_MODEL), jnp.stack(delta_p), jnp.stack(delta_s),
            jnp.stack(qkv_p), jnp.stack(qkv_s), jnp.stack(ffn_p), jnp.stack(ffn_s),
            jnp.stack(vv_s))
```

```python
import functools
import math

import jax
import jax.numpy as jnp
from jax import lax
from jax.experimental import pallas as pl
from jax.experimental.pallas import tpu as pltpu

F32 = jnp.float32
BF16 = jnp.bfloat16

D_MODEL = 1024
DEPTH = 4
H_A = 4
DK = 128
DV = 128
QK_DIM = H_A * DK
V_DIM = H_A * DV
QKV_DIM = 2 * QK_DIM + V_DIM
CONV_A = 4
H_B = 4
D_B = 128
SGU_DIM = H_B * D_B
SGU_CHUNK = 128
D_FF = 2816
CONV_F = 3
N_MOD = 6
EPS = 1e-6

P_Z = QKV_DIM
P_U = P_Z + V_DIM
P_VB = P_U + SGU_DIM
P_SM = P_VB + SGU_DIM
P_DIM = P_SM + 128

LANES = 128
SUBLANES = 8
SUPER = 128
DELTA_C = 64
TB_MIX = 256
TB_FFN = 256
BB_STATE = 8
VMEM_LIMIT = 56 * 1024 * 1024


def _dot(a, b):
    return jnp.dot(a.astype(BF16), b.astype(BF16), preferred_element_type=F32)


def _dot_nt(a, b):
    return lax.dot_general(a.astype(BF16), b.astype(BF16), (((1,), (1,)), ((), ())),
                           preferred_element_type=F32)


def _dot_tn(a, b):
    return lax.dot_general(a.astype(BF16), b.astype(BF16), (((0,), (0,)), ((), ())),
                           preferred_element_type=F32)


def _rms(x, w):
    return x * lax.rsqrt(jnp.mean(x * x, axis=-1, keepdims=True) + EPS) * w


def _l2n(x):
    return x * lax.rsqrt(jnp.sum(x * x, axis=-1, keepdims=True) + EPS)


def _lane_col(x, lane):
    ids = lax.broadcasted_iota(jnp.int32, x.shape, 1)
    return jnp.sum(jnp.where(ids == lane, x, 0.0), axis=1, keepdims=True)


def _decay_logits(sm, alog_row, dtb_row):
    return -jnp.exp(alog_row) * jax.nn.softplus(sm + dtb_row)


def _const_spec(shape):
    nd = len(shape)
    return pl.BlockSpec(shape, lambda *_: (0,) * nd, pipeline_mode=pl.Buffered(1))


def _layer_spec(arr, l):
    shape = tuple(arr.shape[1:])
    return pl.BlockSpec((None,) + shape, lambda *_: (l,) + (0,) * len(shape),
                        pipeline_mode=pl.Buffered(1))


def _mod_kernel(c_ref, w_ref, b_ref, o_ref):
    c_act = jax.nn.silu(c_ref[...])
    o_ref[0] = _dot(c_act, w_ref[0]) + b_ref[0]


def _modulation(c_all, w_mod, b_mod):
    n = c_all.shape[0]
    tn = 1536
    return pl.pallas_call(
        _mod_kernel,
        grid=(DEPTH, (N_MOD * D_MODEL) // tn),
        in_specs=[
            pl.BlockSpec((n, D_MODEL), lambda l, j: (0, 0)),
            pl.BlockSpec((1, D_MODEL, tn), lambda l, j: (l, 0, j)),
            pl.BlockSpec((1, 1, tn), lambda l, j: (l, 0, j)),
        ],
        out_specs=pl.BlockSpec((1, n, tn), lambda l, j: (l, 0, j)),
        out_shape=jax.ShapeDtypeStruct((DEPTH, n, N_MOD * D_MODEL), F32),
        compiler_params=pltpu.CompilerParams(
            dimension_semantics=("arbitrary", "arbitrary"), vmem_limit_bytes=VMEM_LIMIT),
        name="modulation",
    )(c_all, w_mod, b_mod.reshape(DEPTH, 1, N_MOD * D_MODEL))


def _mixer_kernel(x_ref, mod_ref, norm1_ref, w_in_ref, convw_ref, alog_ref, dtb_ref,
                  gdn_ref, sgun_ref, wsgu_ref, bsgu_ref, w_out_ref,
                  xo_ref, s_out_ref, qkvbuf_ref,
                  xp_scr, s_scr, *, tb):
    t = pl.program_id(1)
    nt = pl.num_programs(1)

    @pl.when(t == 0)
    def _():
        xp_scr[0:SUBLANES, :] = jnp.zeros((SUBLANES, QKV_DIM), F32)
        s_scr[...] = jnp.zeros(s_scr.shape, F32)

    x = x_ref[0]
    sh1 = mod_ref[0, :, 0:D_MODEL]
    sc1 = mod_ref[0, :, D_MODEL:2 * D_MODEL]
    g1 = mod_ref[0, :, 2 * D_MODEL:3 * D_MODEL]
    h = _rms(x, norm1_ref[...]) * (1.0 + sc1) + sh1
    proj = _dot(h, w_in_ref[...])

    qkv = proj[:, 0:QKV_DIM]
    xp_scr[SUBLANES:SUBLANES + tb, :] = qkv
    acc = qkv * convw_ref[3:4, :]
    for j in range(CONV_A - 1):
        off = SUBLANES - (CONV_A - 1) + j
        acc = acc + xp_scr[off:off + tb, :] * convw_ref[j:j + 1, :]
    qkvbuf_ref[0] = xp_scr[tb + SUBLANES - (CONV_A - 1):tb + SUBLANES, :]
    xp_scr[0:SUBLANES, :] = xp_scr[tb:tb + SUBLANES, :]
    qkvc = jax.nn.silu(acc)

    sm = proj[:, P_SM:P_SM + LANES]
    beta_all = jax.nn.sigmoid(sm)
    g_all = _decay_logits(sm, alog_ref[...], dtb_ref[...])

    ri = lax.broadcasted_iota(jnp.int32, (tb, tb), 0)
    ci = lax.broadcasted_iota(jnp.int32, (tb, tb), 1)
    cum_mat = jnp.where((ri // DELTA_C == ci // DELTA_C) & (ri >= ci), 1.0, 0.0).astype(BF16)
    g_hi = g_all.astype(BF16)
    r1 = g_all - g_hi.astype(F32)
    g_mid = r1.astype(BF16)
    g_lo = (r1 - g_mid.astype(F32)).astype(BF16)
    gc_all = (jnp.dot(cum_mat, g_hi, preferred_element_type=F32)
              + jnp.dot(cum_mat, g_mid, preferred_element_type=F32)
              + jnp.dot(cum_mat, g_lo, preferred_element_type=F32))
    gc_t = gc_all.T

    rs = lax.broadcasted_iota(jnp.int32, (SUPER, SUPER), 0)
    cs = lax.broadcasted_iota(jnp.int32, (SUPER, SUPER), 1)
    same_chunk = (rs // DELTA_C) == (cs // DELTA_C)
    tril = same_chunk & (rs >= cs)
    strict = same_chunk & (rs > cs)
    eye = jnp.where(rs == cs, 1.0, 0.0)
    pair_mask = (rs // 2 == cs // 2) & (rs > cs)
    quad_masks = []
    half = 2
    while half < DELTA_C:
        quad_masks.append((rs // (2 * half) == cs // (2 * half))
                          & (rs % (2 * half) >= half) & (cs % (2 * half) < half))
        half *= 2
    chunks_per_super = SUPER // DELTA_C

    row_chunk = lax.broadcasted_iota(jnp.int32, (SUPER, 1), 0) // DELTA_C
    n_super = tb // SUPER

    pairs = [(hh, s) for s in range(n_super) for hh in range(H_A)]
    head_in = []
    for hh in range(H_A):
        head_in.append((
            _l2n(qkvc[:, hh * DK:(hh + 1) * DK]) * (DK ** -0.5),
            _l2n(qkvc[:, QK_DIM + hh * DK:QK_DIM + (hh + 1) * DK]),
            qkvc[:, 2 * QK_DIM + hh * DV:2 * QK_DIM + (hh + 1) * DV],
            _lane_col(beta_all, hh),
            _lane_col(gc_all, H_A + hh)))

    low_p, qk_p, rhs_p, qg_p, kdec_p, glast_p = {}, {}, {}, {}, {}, {}
    for p in pairs:
        hh, s = p
        q_h, k_h, v_h, beta_c, gc_c = head_in[hh]
        r0 = s * SUPER
        qs = q_h[r0:r0 + SUPER]
        ks = k_h[r0:r0 + SUPER]
        bs = beta_c[r0:r0 + SUPER]
        gcol = gc_c[r0:r0 + SUPER]
        grow = gc_t[H_A + hh:H_A + hh + 1, r0:r0 + SUPER]
        decay = jnp.exp(jnp.where(tril, gcol - grow, -jnp.inf))
        kb = ks * bs
        eg = jnp.exp(gcol)
        low_p[p] = jnp.where(strict, _dot_nt(kb, ks) * decay, 0.0)
        qk_p[p] = jnp.where(tril, _dot_nt(qs, ks) * decay, 0.0)
        rhs_p[p] = jnp.concatenate([v_h[r0:r0 + SUPER] * bs, kb * eg], axis=1)
        qg_p[p] = qs * eg
        g_last = [grow[:, (c + 1) * DELTA_C - 1:(c + 1) * DELTA_C] for c in range(chunks_per_super)]
        glast_col = g_last[-1]
        for c in range(chunks_per_super - 2, -1, -1):
            glast_col = jnp.where(row_chunk == c, g_last[c], glast_col)
        kdec_p[p] = ks * jnp.exp(glast_col - gcol)
        glast_p[p] = g_last

    inv_p = {p: eye - jnp.where(pair_mask, low_p[p], 0.0) for p in pairs}
    for qm in quad_masks:
        t_p = {p: _dot(inv_p[p], jnp.where(qm, low_p[p], 0.0)) for p in pairs}
        inv_p = {p: inv_p[p] - _dot(t_p[p], inv_p[p]) for p in pairs}
    sol_p = {p: _dot(inv_p[p], rhs_p[p]) for p in pairs}
    qs_p = {p: _dot(qk_p[p], sol_p[p]) for p in pairs}
    bm_p = {(p, c): _dot_tn(kdec_p[p][c * DELTA_C:(c + 1) * DELTA_C],
                            sol_p[p][c * DELTA_C:(c + 1) * DELTA_C])
            for p in pairs for c in range(chunks_per_super)}

    state = [s_scr[hh] for hh in range(H_A)]
    o_rows = [[] for _ in range(H_A)]
    for s in range(n_super):
        for c in range(chunks_per_super):
            c0 = c * DELTA_C
            for hh in range(H_A):
                p = (hh, s)
                s_b = state[hh].astype(BF16)
                q_eff = qg_p[p][c0:c0 + DELTA_C] - qs_p[p][c0:c0 + DELTA_C, DV:DV + DK]
                o_rows[hh].append(_dot(q_eff, s_b) + qs_p[p][c0:c0 + DELTA_C, 0:DV])
                bm = bm_p[(p, c)]
                state[hh] = (state[hh] * jnp.exp(glast_p[p][c]) - _dot(bm[:, DV:DV + DK], s_b)
                             + bm[:, 0:DV])
    o_heads = []
    for hh in range(H_A):
        s_scr[hh] = state[hh]
        o_h = jnp.concatenate(o_rows[hh], axis=0)
        z_h = proj[:, P_Z + hh * DV:P_Z + (hh + 1) * DV]
        o_heads.append(_rms(o_h, gdn_ref[...]) * jax.nn.silu(z_h))

    rc = lax.broadcasted_iota(jnp.int32, (SGU_CHUNK, SGU_CHUNK), 0)
    cc = lax.broadcasted_iota(jnp.int32, (SGU_CHUNK, SGU_CHUNK), 1)
    for hh in range(H_B):
        u_h = jax.nn.gelu(proj[:, P_U + hh * D_B:P_U + (hh + 1) * D_B])
        v_h = _rms(jax.nn.gelu(proj[:, P_VB + hh * D_B:P_VB + (hh + 1) * D_B]),
                   sgun_ref[:, hh * D_B:(hh + 1) * D_B])
        w_h = jnp.where(rc >= cc, wsgu_ref[hh], 0.0)
        b_col = _lane_col(bsgu_ref[...], hh)
        z_rows = []
        for s in range(tb // SGU_CHUNK):
            r0 = s * SGU_CHUNK
            z_rows.append(_dot(w_h, v_h[r0:r0 + SGU_CHUNK]) + b_col)
        o_heads.append(u_h * jnp.concatenate(z_rows, axis=0))

    mix = jnp.concatenate(o_heads, axis=1)
    xo_ref[0] = x + g1 * _dot(mix, w_out_ref[...])

    @pl.when(t == nt - 1)
    def _():
        s_out_ref[0] = s_scr[...]


def _prompt_mixer(l, x, mod, norm1, w_in, convw, alog_row, dtb_row, gdn, sgun, wsgu, bsgu_t, w_out):
    nb, seq, _ = x.shape
    tb = TB_MIX
    kern = functools.partial(_mixer_kernel, tb=tb)
    consts = (norm1, w_in, convw, alog_row, dtb_row, gdn, sgun, wsgu, bsgu_t, w_out)
    return pl.pallas_call(
        kern,
        grid=(nb, seq // tb),
        in_specs=[
            pl.BlockSpec((1, tb, D_MODEL), lambda b, t: (b, t, 0)),
            pl.BlockSpec((None, 1, 1, N_MOD * D_MODEL), lambda b, t: (l, b, 0, 0)),
        ] + [_layer_spec(a, l) for a in consts],
        out_specs=[
            pl.BlockSpec((1, tb, D_MODEL), lambda b, t: (b, t, 0)),
            pl.BlockSpec((1, H_A, DK, DV), lambda b, t: (b, 0, 0, 0)),
            pl.BlockSpec((1, CONV_A - 1, QKV_DIM), lambda b, t: (b, 0, 0)),
        ],
        out_shape=[
            jax.ShapeDtypeStruct((nb, seq, D_MODEL), F32),
            jax.ShapeDtypeStruct((nb, H_A, DK, DV), F32),
            jax.ShapeDtypeStruct((nb, CONV_A - 1, QKV_DIM), F32),
        ],
        scratch_shapes=[
            pltpu.VMEM((tb + SUBLANES, QKV_DIM), F32),
            pltpu.VMEM((H_A, DK, DV), F32),
        ],
        compiler_params=pltpu.CompilerParams(
            dimension_semantics=("arbitrary", "arbitrary"), vmem_limit_bytes=VMEM_LIMIT),
        name="prompt_mixer",
    )(x, mod, norm1, w_in, convw, alog_row, dtb_row, gdn, sgun, wsgu, bsgu_t, w_out)


def _ffn_body(x, mod_row, norm2, w_up_ref, upc_fn, w_down_ref):
    sh2 = mod_row[:, 3 * D_MODEL:4 * D_MODEL]
    sc2 = mod_row[:, 4 * D_MODEL:5 * D_MODEL]
    g2 = mod_row[:, 5 * D_MODEL:6 * D_MODEL]
    h2 = _rms(x, norm2) * (1.0 + sc2) + sh2
    up = _dot(h2, w_up_ref[...])
    upc = upc_fn(up)
    gated = jax.nn.silu(upc[:, 0:D_FF]) * upc[:, D_FF:2 * D_FF]
    return x + g2 * _dot(gated, w_down_ref[...])


def _ffn_kernel(x_ref, mod_ref, norm2_ref, w_up_ref, convw_ref, convb_ref, w_down_ref, fnorm_ref,
                xo_ref, ffnbuf_ref, xp_scr, *, tb, final):
    t = pl.program_id(1)

    @pl.when(t == 0)
    def _():
        xp_scr[0:SUBLANES, :] = jnp.zeros((SUBLANES, 2 * D_FF), F32)

    def conv(up):
        xp_scr[SUBLANES:SUBLANES + tb, :] = up
        acc = up * convw_ref[CONV_F - 1:CONV_F, :] + convb_ref[...]
        for j in range(CONV_F - 1):
            off = SUBLANES - (CONV_F - 1) + j
            acc = acc + xp_scr[off:off + tb, :] * convw_ref[j:j + 1, :]
        ffnbuf_ref[0] = xp_scr[tb + SUBLANES - (CONV_F - 1):tb + SUBLANES, :]
        xp_scr[0:SUBLANES, :] = xp_scr[tb:tb + SUBLANES, :]
        return acc

    x_new = _ffn_body(x_ref[0], mod_ref[0], norm2_ref[...], w_up_ref, conv, w_down_ref)
    if final:
        x_new = _rms(x_new, fnorm_ref[...])
    xo_ref[0] = x_new


def _prompt_ffn(l, x, mod, norm2, w_up, convw, convb, w_down, fnorm, final):
    nb, seq, _ = x.shape
    tb = TB_FFN
    kern = functools.partial(_ffn_kernel, tb=tb, final=final)
    return pl.pallas_call(
        kern,
        grid=(nb, seq // tb),
        in_specs=[
            pl.BlockSpec((1, tb, D_MODEL), lambda b, t: (b, t, 0)),
            pl.BlockSpec((None, 1, 1, N_MOD * D_MODEL), lambda b, t: (l, b, 0, 0)),
        ] + [_layer_spec(a, l) for a in (norm2, w_up, convw, convb, w_down)] + [
            _const_spec((1, D_MODEL)),
        ],
        out_specs=[
            pl.BlockSpec((1, tb, D_MODEL), lambda b, t: (b, t, 0)),
            pl.BlockSpec((1, CONV_F - 1, 2 * D_FF), lambda b, t: (b, 0, 0)),
        ],
        out_shape=[
            jax.ShapeDtypeStruct((nb, seq, D_MODEL), F32),
            jax.ShapeDtypeStruct((nb, CONV_F - 1, 2 * D_FF), F32),
        ],
        scratch_shapes=[pltpu.VMEM((tb + SUBLANES, 2 * D_FF), F32)],
        compiler_params=pltpu.CompilerParams(
            dimension_semantics=("arbitrary", "arbitrary"), vmem_limit_bytes=VMEM_LIMIT),
        name="prompt_ffn",
    )(x, mod, norm2, w_up, convw, convb, w_down, fnorm)


def _bcast_heads(cols):
    return jnp.concatenate([jnp.broadcast_to(c, (c.shape[0], LANES)) for c in cols], axis=1)


def _sample_in_kernel(x_ref, mod_ref, norm1_ref, w_in_ref, cstate_ref, convw_ref, alog_ref, dtb_ref,
                      sgun_ref, wd_ref, b0_ref,
                      ke_ref, qe_ref, kn_ref, v_ref, beta_ref, eg_ref, qk_ref, z_ref, ob_ref, vv_ref,
                      cnew_ref):
    x = x_ref[...]
    sh1 = mod_ref[:, 0:D_MODEL]
    sc1 = mod_ref[:, D_MODEL:2 * D_MODEL]
    h = _rms(x, norm1_ref[...]) * (1.0 + sc1) + sh1
    proj = _dot(h, w_in_ref[...])
    qkv = proj[:, 0:QKV_DIM]
    acc = qkv * convw_ref[CONV_A - 1:CONV_A, :]
    for j in range(CONV_A - 1):
        acc = acc + cstate_ref[:, j * QKV_DIM:(j + 1) * QKV_DIM] * convw_ref[j:j + 1, :]
    for j in range(CONV_A - 2):
        cnew_ref[:, j * QKV_DIM:(j + 1) * QKV_DIM] = cstate_ref[:, (j + 1) * QKV_DIM:(j + 2) * QKV_DIM]
    cnew_ref[:, (CONV_A - 2) * QKV_DIM:(CONV_A - 1) * QKV_DIM] = qkv
    qkvc = jax.nn.silu(acc)

    sm = proj[:, P_SM:P_SM + LANES]
    beta_all = jax.nn.sigmoid(sm)
    eg_all = jnp.exp(_decay_logits(sm, alog_ref[...], dtb_ref[...]))
    beta_cols, eg_cols, qk_cols = [], [], []
    for hh in range(H_A):
        q_h = _l2n(qkvc[:, hh * DK:(hh + 1) * DK]) * (DK ** -0.5)
        k_h = _l2n(qkvc[:, QK_DIM + hh * DK:QK_DIM + (hh + 1) * DK])
        eg_c = _lane_col(eg_all, H_A + hh)
        sl = slice(hh * DK, (hh + 1) * DK)
        ke_ref[:, sl] = k_h * eg_c
        qe_ref[:, sl] = q_h * eg_c
        kn_ref[:, sl] = k_h
        beta_cols.append(_lane_col(beta_all, hh))
        eg_cols.append(eg_c)
        qk_cols.append(jnp.sum(q_h * k_h, axis=1, keepdims=True))
    v_ref[...] = qkvc[:, 2 * QK_DIM:2 * QK_DIM + V_DIM]
    beta_ref[...] = _bcast_heads(beta_cols)
    eg_ref[...] = _bcast_heads(eg_cols)
    qk_ref[...] = _bcast_heads(qk_cols)
    z_ref[...] = proj[:, P_Z:P_Z + V_DIM]

    vv_parts = []
    for hh in range(H_B):
        vv_parts.append(_rms(jax.nn.gelu(proj[:, P_VB + hh * D_B:P_VB + (hh + 1) * D_B]),
                             sgun_ref[:, hh * D_B:(hh + 1) * D_B]))
    vv = jnp.concatenate(vv_parts, axis=1)
    vv_ref[...] = vv
    ob_ref[...] = jax.nn.gelu(proj[:, P_U:P_U + SGU_DIM]) * (wd_ref[...] * vv + b0_ref[...])


def _sample_in(l, x, mod, norm1, w_in, cstate, convw, alog_row, dtb_row, sgun, wd_row, b0_row):
    n = x.shape[0]
    wide = jax.ShapeDtypeStruct((n, V_DIM), F32)
    stacked = (mod, norm1, w_in, cstate, convw, alog_row, dtb_row, sgun, wd_row, b0_row)
    outs = [wide] * 10 + [jax.ShapeDtypeStruct((n, (CONV_A - 1) * QKV_DIM), F32)]
    return pl.pallas_call(
        _sample_in_kernel,
        grid=(1,),
        in_specs=[_const_spec(x.shape)] + [_layer_spec(a, l) for a in stacked],
        out_specs=[pl.BlockSpec(o.shape, lambda i: (0, 0)) for o in outs],
        out_shape=outs,
        compiler_params=pltpu.CompilerParams(
            dimension_semantics=("arbitrary",), vmem_limit_bytes=VMEM_LIMIT),
        name="sample_in",
    )(x, *stacked)


def _sample_state_kernel(s_ref, ke_ref, qe_ref, kn_ref, v_ref, beta_ref, eg_ref, qk_ref, *rest):
    s_out_ref, o_ref = rest[-2:]
    bb = BB_STATE
    row = lax.broadcasted_iota(jnp.int32, (bb, DK), 0)
    for hh in range(H_A):
        sl = slice(hh * DK, (hh + 1) * DK)
        lhs = jnp.concatenate([ke_ref[:, sl], qe_ref[:, sl]], axis=0)
        pred_k = jnp.zeros((bb, DV), F32)
        pred_q = jnp.zeros((bb, DV), F32)
        for i in range(bb):
            r = jnp.dot(lhs, s_ref[i, hh], preferred_element_type=F32)
            pred_k = pred_k + jnp.where(row == i, r[0:bb], 0.0)
            pred_q = pred_q + jnp.where(row == i, r[bb:2 * bb], 0.0)
        v_new = beta_ref[:, sl] * (v_ref[:, sl] - pred_k)
        o_ref[:, sl] = pred_q + qk_ref[:, sl] * v_new
        kn = kn_ref[:, sl]
        eg = eg_ref[:, sl]
        for i in range(bb):
            upd = lax.dot_general(jnp.where(row == i, kn, 0.0), v_new, (((0,), (0,)), ((), ())),
                                  preferred_element_type=F32)
            s_out_ref[i, hh] = s_ref[i, hh] * eg[i:i + 1, :] + upd


def _sample_state(l, s_all, s_new_all, ke, qe, kn, v, beta, eg, qk):
    n = s_all.shape[1]
    bb = BB_STATE
    row_spec = pl.BlockSpec((bb, V_DIM), lambda i: (i, 0))
    s_spec = pl.BlockSpec((None, bb, H_A, DK, DV), lambda i: (l, i, 0, 0, 0))
    ins = [s_all, ke, qe, kn, v, beta, eg, qk]
    in_specs = [s_spec] + [row_spec] * 7
    aliases = {}
    if s_new_all is not None:
        ins.append(s_new_all)
        in_specs.append(pl.BlockSpec(memory_space=pl.ANY))
        aliases = {len(ins) - 1: 0}
    return pl.pallas_call(
        _sample_state_kernel,
        grid=(n // bb,),
        in_specs=in_specs,
        out_specs=[s_spec, row_spec],
        out_shape=[jax.ShapeDtypeStruct(s_all.shape, F32), jax.ShapeDtypeStruct((n, V_DIM), F32)],
        input_output_aliases=aliases,
        compiler_params=pltpu.CompilerParams(
            dimension_semantics=("arbitrary",), vmem_limit_bytes=VMEM_LIMIT),
        name="sample_state",
    )(*ins)


def _sample_out_kernel(x_ref, mod_ref, o_ref, z_ref, ob_ref, gdn_ref, w_out_ref, norm2_ref, w_up_ref,
                       fstate_ref, convw_ref, convb_ref, w_down_ref, fnorm_ref,
                       xo_ref, fnew_ref, *, final):
    x = x_ref[...]
    g1 = mod_ref[:, 2 * D_MODEL:3 * D_MODEL]
    parts = []
    for hh in range(H_A):
        sl = slice(hh * DV, (hh + 1) * DV)
        parts.append(_rms(o_ref[:, sl], gdn_ref[...]) * jax.nn.silu(z_ref[:, sl]))
    parts.append(ob_ref[...])
    mix = jnp.concatenate(parts, axis=1)
    x = x + g1 * _dot(mix, w_out_ref[...])

    def conv(up):
        acc = up * convw_ref[CONV_F - 1:CONV_F, :] + convb_ref[...]
        for j in range(CONV_F - 1):
            acc = acc + fstate_ref[:, j * 2 * D_FF:(j + 1) * 2 * D_FF] * convw_ref[j:j + 1, :]
        for j in range(CONV_F - 2):
            fnew_ref[:, j * 2 * D_FF:(j + 1) * 2 * D_FF] = fstate_ref[:, (j + 1) * 2 * D_FF:(j + 2) * 2 * D_FF]
        fnew_ref[:, (CONV_F - 2) * 2 * D_FF:(CONV_F - 1) * 2 * D_FF] = up
        return acc

    x_new = _ffn_body(x, mod_ref[...], norm2_ref[...], w_up_ref, conv, w_down_ref)
    if final:
        x_new = _rms(x_new, fnorm_ref[...])
    xo_ref[...] = x_new


def _sample_out(l, x, mod, o, z, ob, gdn, w_out, norm2, w_up, fstate, convw, convb, w_down, fnorm,
                final):
    n = x.shape[0]
    ins = (x, mod, o, z, ob, gdn, w_out, norm2, w_up, fstate, convw, convb, w_down, fnorm)
    per_layer = (False, True, False, False, False, True, True, True, True, True, True, True, True, False)
    outs = [jax.ShapeDtypeStruct((n, D_MODEL), F32),
            jax.ShapeDtypeStruct((n, (CONV_F - 1) * 2 * D_FF), F32)]
    return pl.pallas_call(
        functools.partial(_sample_out_kernel, final=final),
        grid=(1,),
        in_specs=[_layer_spec(a, l) if st else _const_spec(a.shape) for a, st in zip(ins, per_layer)],
        out_specs=[pl.BlockSpec(o_.shape, lambda i: (0, 0)) for o_ in outs],
        out_shape=outs,
        compiler_params=pltpu.CompilerParams(
            dimension_semantics=("arbitrary",), vmem_limit_bytes=VMEM_LIMIT),
        name="sample_out",
    )(*ins)


def _lane_rows(vals, offset):
    return jnp.zeros((vals.shape[0], 1, LANES), F32).at[:, 0, offset:offset + vals.shape[1]].set(vals)


def kernel(x_prompt, x_sample, state_delta, state_qkv_conv, state_ffn_conv, c_prompt, c_sample,
           w_mod, b_mod, norm1, w_in, conv_qkv, a_log, dt_bias, gdn_norm, sgu_norm, w_sgu, b_sgu,
           w_out, norm2, w_up, conv_ffn_w, conv_ffn_b, w_down, final_norm):
    nbp = x_prompt.shape[0]
    nbs = x_sample.shape[0]

    off_z = QKV_DIM
    off_beta = off_z + V_DIM
    off_u = off_beta + 2 * H_A
    off_vb = off_u + SGU_DIM
    w_in_p = jnp.concatenate(
        [w_in[:, :, 0:off_beta], w_in[:, :, off_u:off_vb + SGU_DIM], w_in[:, :, off_beta:off_u],
         jnp.zeros((DEPTH, D_MODEL, LANES - 2 * H_A), w_in.dtype)], axis=2).astype(BF16)
    w_out_b = w_out.astype(BF16)
    w_up_b = w_up.astype(BF16)
    w_down_b = w_down.astype(BF16)

    mod_all = _modulation(jnp.concatenate([c_prompt, c_sample], axis=0), w_mod, b_mod)
    mod_p = mod_all[:, :nbp].reshape(DEPTH, nbp, 1, N_MOD * D_MODEL)
    mod_s = mod_all[:, nbp:]
    fnorm = final_norm.reshape(1, D_MODEL)

    n1 = norm1.reshape(DEPTH, 1, D_MODEL)
    n2 = norm2.reshape(DEPTH, 1, D_MODEL)
    alog_rows = _lane_rows(a_log, H_A)
    dtb_rows = _lane_rows(dt_bias, H_A)
    gdn = gdn_norm.reshape(DEPTH, 1, DV)
    sgun = sgu_norm.reshape(DEPTH, 1, SGU_DIM)
    convb = conv_ffn_b.reshape(DEPTH, 1, 2 * D_FF)
    bsgu_t = jnp.zeros((DEPTH, SGU_CHUNK, LANES), F32).at[:, :, 0:H_B].set(
        jnp.swapaxes(b_sgu, 1, 2))
    wd_rows = jnp.repeat(w_sgu[:, :, 0, 0], D_B, axis=1).reshape(DEPTH, 1, SGU_DIM)
    b0_rows = jnp.repeat(b_sgu[:, :, 0], D_B, axis=1).reshape(DEPTH, 1, SGU_DIM)

    xp = x_prompt
    xs = x_sample.reshape(nbs, D_MODEL)
    cstate = state_qkv_conv.reshape(DEPTH, nbs, (CONV_A - 1) * QKV_DIM)
    fstate = state_ffn_conv.reshape(DEPTH, nbs, (CONV_F - 1) * 2 * D_FF)
    delta_s = None
    delta_p, qkv_p, qkv_s, ffn_p, ffn_s, vv_s = [], [], [], [], [], []
    for l in range(DEPTH):
        final = l == DEPTH - 1
        xp, s_p, qb_p = _prompt_mixer(l, xp, mod_p, n1, w_in_p, conv_qkv, alog_rows, dtb_rows,
                                      gdn, sgun, w_sgu, bsgu_t, w_out_b)
        xp, fb_p = _prompt_ffn(l, xp, mod_p, n2, w_up_b, conv_ffn_w, convb, w_down_b, fnorm, final)
        delta_p.append(s_p)
        qkv_p.append(qb_p)
        ffn_p.append(fb_p)

        (ke, qe, kn, v, beta, eg, qk, z, ob, vv, c_new) = _sample_in(
            l, xs, mod_s, n1, w_in_p, cstate, conv_qkv, alog_rows, dtb_rows, sgun, wd_rows, b0_rows)
        delta_s, o = _sample_state(l, state_delta, delta_s, ke, qe, kn, v, beta, eg, qk)
        xs, f_new = _sample_out(l, xs, mod_s, o, z, ob, gdn, w_out_b, n2, w_up_b, fstate,
                                conv_ffn_w, convb, w_down_b, fnorm, final)
        qkv_s.append(c_new.reshape(nbs, CONV_A - 1, QKV_DIM))
        ffn_s.append(f_new.reshape(nbs, CONV_F - 1, 2 * D_FF))
        vv_s.append(vv.reshape(nbs, 1, H_B, D_B))

    return (xp, xs.reshape(nbs, 1, D_MODEL), jnp.stack(delta_p), delta_s,
            jnp.stack(qkv_p), jnp.stack(qkv_s), jnp.stack(ffn_p), jnp.stack(ffn_s),
            jnp.stack(vv_s))
```

```python
import functools
import math

import jax
import jax.numpy as jnp
from jax import lax
from jax.experimental import pallas as pl
from jax.experimental.pallas import tpu as pltpu

F32 = jnp.float32
BF16 = jnp.bfloat16

D_MODEL = 1024
DEPTH = 4
H_A = 4
DK = 128
DV = 128
QK_DIM = H_A * DK
V_DIM = H_A * DV
QKV_DIM = 2 * QK_DIM + V_DIM
CONV_A = 4
H_B = 4
D_B = 128
SGU_DIM = H_B * D_B
SGU_CHUNK = 128
D_FF = 2816
CONV_F = 3
N_MOD = 6
EPS = 1e-6

P_Z = QKV_DIM
P_U = P_Z + V_DIM
P_VB = P_U + SGU_DIM
P_SM = P_VB + SGU_DIM
P_DIM = P_SM + 128

LANES = 128
SUBLANES = 8
SUPER = 128
DELTA_C = 64
TB_MIX = 512
TB_FFN = 256
BB_STATE = 8
VMEM_LIMIT = 56 * 1024 * 1024


def _dot(a, b):
    return jnp.dot(a.astype(BF16), b.astype(BF16), preferred_element_type=F32)


def _dot_nt(a, b):
    return lax.dot_general(a.astype(BF16), b.astype(BF16), (((1,), (1,)), ((), ())),
                           preferred_element_type=F32)


def _dot_tn(a, b):
    return lax.dot_general(a.astype(BF16), b.astype(BF16), (((0,), (0,)), ((), ())),
                           preferred_element_type=F32)


def _rms(x, w):
    return x * lax.rsqrt(jnp.mean(x * x, axis=-1, keepdims=True) + EPS) * w


def _l2n(x):
    return x * lax.rsqrt(jnp.sum(x * x, axis=-1, keepdims=True) + EPS)


def _lane_col(x, lane):
    ids = lax.broadcasted_iota(jnp.int32, x.shape, 1)
    return jnp.sum(jnp.where(ids == lane, x, 0.0), axis=1, keepdims=True)


def _decay_logits(sm, alog_row, dtb_row):
    return -jnp.exp(alog_row) * jax.nn.softplus(sm + dtb_row)


def _const_spec(shape):
    nd = len(shape)
    return pl.BlockSpec(shape, lambda *_: (0,) * nd, pipeline_mode=pl.Buffered(1))


def _layer_spec(arr, l):
    shape = tuple(arr.shape[1:])
    return pl.BlockSpec((None,) + shape, lambda *_: (l,) + (0,) * len(shape),
                        pipeline_mode=pl.Buffered(1))


def _mod_kernel(c_ref, w_ref, b_ref, o_ref):
    c_act = jax.nn.silu(c_ref[...])
    o_ref[0] = _dot(c_act, w_ref[0]) + b_ref[0]


def _modulation(c_all, w_mod, b_mod):
    n = c_all.shape[0]
    tn = 1536
    return pl.pallas_call(
        _mod_kernel,
        grid=(DEPTH, (N_MOD * D_MODEL) // tn),
        in_specs=[
            pl.BlockSpec((n, D_MODEL), lambda l, j: (0, 0)),
            pl.BlockSpec((1, D_MODEL, tn), lambda l, j: (l, 0, j)),
            pl.BlockSpec((1, 1, tn), lambda l, j: (l, 0, j)),
        ],
        out_specs=pl.BlockSpec((1, n, tn), lambda l, j: (l, 0, j)),
        out_shape=jax.ShapeDtypeStruct((DEPTH, n, N_MOD * D_MODEL), F32),
        compiler_params=pltpu.CompilerParams(
            dimension_semantics=("arbitrary", "arbitrary"), vmem_limit_bytes=VMEM_LIMIT),
        name="modulation",
    )(c_all, w_mod, b_mod.reshape(DEPTH, 1, N_MOD * D_MODEL))


def _mixer_kernel(x_ref, mod_ref, norm1_ref, w_in_ref, convw_ref, alog_ref, dtb_ref,
                  gdn_ref, sgun_ref, wsgu_ref, bsgu_ref, w_out_ref,
                  xo_ref, s_out_ref, qkvbuf_ref,
                  xp_scr, s_scr, *, tb):
    t = pl.program_id(1)
    nt = pl.num_programs(1)

    @pl.when(t == 0)
    def _():
        xp_scr[0:SUBLANES, :] = jnp.zeros((SUBLANES, QKV_DIM), F32)
        s_scr[...] = jnp.zeros(s_scr.shape, F32)

    x = x_ref[0]
    sh1 = mod_ref[0, :, 0:D_MODEL]
    sc1 = mod_ref[0, :, D_MODEL:2 * D_MODEL]
    g1 = mod_ref[0, :, 2 * D_MODEL:3 * D_MODEL]
    h = _rms(x, norm1_ref[...]) * (1.0 + sc1) + sh1
    proj = _dot(h, w_in_ref[...])

    qkv = proj[:, 0:QKV_DIM]
    xp_scr[SUBLANES:SUBLANES + tb, :] = qkv
    acc = qkv * convw_ref[3:4, :]
    for j in range(CONV_A - 1):
        off = SUBLANES - (CONV_A - 1) + j
        acc = acc + xp_scr[off:off + tb, :] * convw_ref[j:j + 1, :]
    qkvbuf_ref[0] = xp_scr[tb + SUBLANES - (CONV_A - 1):tb + SUBLANES, :]
    xp_scr[0:SUBLANES, :] = xp_scr[tb:tb + SUBLANES, :]
    qkvc = jax.nn.silu(acc)

    sm = proj[:, P_SM:P_SM + LANES]
    beta_all = jax.nn.sigmoid(sm)
    g_all = _decay_logits(sm, alog_ref[...], dtb_ref[...])

    ri = lax.broadcasted_iota(jnp.int32, (tb, tb), 0)
    ci = lax.broadcasted_iota(jnp.int32, (tb, tb), 1)
    cum_mat = jnp.where((ri // DELTA_C == ci // DELTA_C) & (ri >= ci), 1.0, 0.0).astype(BF16)
    g_hi = g_all.astype(BF16)
    r1 = g_all - g_hi.astype(F32)
    g_mid = r1.astype(BF16)
    g_lo = (r1 - g_mid.astype(F32)).astype(BF16)
    gc_all = (jnp.dot(cum_mat, g_hi, preferred_element_type=F32)
              + jnp.dot(cum_mat, g_mid, preferred_element_type=F32)
              + jnp.dot(cum_mat, g_lo, preferred_element_type=F32))
    gc_t = gc_all.T

    rs = lax.broadcasted_iota(jnp.int32, (SUPER, SUPER), 0)
    cs = lax.broadcasted_iota(jnp.int32, (SUPER, SUPER), 1)
    same_chunk = (rs // DELTA_C) == (cs // DELTA_C)
    tril = same_chunk & (rs >= cs)
    strict = same_chunk & (rs > cs)
    eye = jnp.where(rs == cs, 1.0, 0.0)
    pair_mask = (rs // 2 == cs // 2) & (rs > cs)
    quad_masks = []
    half = 2
    while half < DELTA_C:
        quad_masks.append((rs // (2 * half) == cs // (2 * half))
                          & (rs % (2 * half) >= half) & (cs % (2 * half) < half))
        half *= 2
    chunks_per_super = SUPER // DELTA_C

    row_chunk = lax.broadcasted_iota(jnp.int32, (SUPER, 1), 0) // DELTA_C
    n_super = tb // SUPER

    pairs = [(hh, s) for s in range(n_super) for hh in range(H_A)]
    head_in = []
    for hh in range(H_A):
        head_in.append((
            _l2n(qkvc[:, hh * DK:(hh + 1) * DK]) * (DK ** -0.5),
            _l2n(qkvc[:, QK_DIM + hh * DK:QK_DIM + (hh + 1) * DK]),
            qkvc[:, 2 * QK_DIM + hh * DV:2 * QK_DIM + (hh + 1) * DV],
            _lane_col(beta_all, hh),
            _lane_col(gc_all, H_A + hh)))

    low_p, qk_p, rhs_p, qg_p, kdec_p, glast_p = {}, {}, {}, {}, {}, {}
    for p in pairs:
        hh, s = p
        q_h, k_h, v_h, beta_c, gc_c = head_in[hh]
        r0 = s * SUPER
        qs = q_h[r0:r0 + SUPER]
        ks = k_h[r0:r0 + SUPER]
        bs = beta_c[r0:r0 + SUPER]
        gcol = gc_c[r0:r0 + SUPER]
        grow = gc_t[H_A + hh:H_A + hh + 1, r0:r0 + SUPER]
        decay = jnp.exp(jnp.where(tril, gcol - grow, -jnp.inf))
        kb = ks * bs
        eg = jnp.exp(gcol)
        low_p[p] = jnp.where(strict, _dot_nt(kb, ks) * decay, 0.0)
        qk_p[p] = jnp.where(tril, _dot_nt(qs, ks) * decay, 0.0)
        rhs_p[p] = jnp.concatenate([v_h[r0:r0 + SUPER] * bs, kb * eg], axis=1)
        qg_p[p] = qs * eg
        g_last = [grow[:, (c + 1) * DELTA_C - 1:(c + 1) * DELTA_C] for c in range(chunks_per_super)]
        glast_col = g_last[-1]
        for c in range(chunks_per_super - 2, -1, -1):
            glast_col = jnp.where(row_chunk == c, g_last[c], glast_col)
        kdec_p[p] = ks * jnp.exp(glast_col - gcol)
        glast_p[p] = g_last

    inv_p = {p: eye - jnp.where(pair_mask, low_p[p], 0.0) for p in pairs}
    for qm in quad_masks:
        t_p = {p: _dot(inv_p[p], jnp.where(qm, low_p[p], 0.0)) for p in pairs}
        inv_p = {p: inv_p[p] - _dot(t_p[p], inv_p[p]) for p in pairs}
    sol_p = {p: _dot(inv_p[p], rhs_p[p]) for p in pairs}
    qs_p = {p: _dot(qk_p[p], sol_p[p]) for p in pairs}
    bm_p = {(p, c): _dot_tn(kdec_p[p][c * DELTA_C:(c + 1) * DELTA_C],
                            sol_p[p][c * DELTA_C:(c + 1) * DELTA_C])
            for p in pairs for c in range(chunks_per_super)}

    state = [s_scr[hh] for hh in range(H_A)]
    o_rows = [[] for _ in range(H_A)]
    for s in range(n_super):
        for c in range(chunks_per_super):
            c0 = c * DELTA_C
            for hh in range(H_A):
                p = (hh, s)
                s_b = state[hh].astype(BF16)
                q_eff = qg_p[p][c0:c0 + DELTA_C] - qs_p[p][c0:c0 + DELTA_C, DV:DV + DK]
                o_rows[hh].append(_dot(q_eff, s_b) + qs_p[p][c0:c0 + DELTA_C, 0:DV])
                bm = bm_p[(p, c)]
                state[hh] = (state[hh] * jnp.exp(glast_p[p][c]) - _dot(bm[:, DV:DV + DK], s_b)
                             + bm[:, 0:DV])
    o_heads = []
    for hh in range(H_A):
        s_scr[hh] = state[hh]
        o_h = jnp.concatenate(o_rows[hh], axis=0)
        z_h = proj[:, P_Z + hh * DV:P_Z + (hh + 1) * DV]
        o_heads.append(_rms(o_h, gdn_ref[...]) * jax.nn.silu(z_h))

    rc = lax.broadcasted_iota(jnp.int32, (SGU_CHUNK, SGU_CHUNK), 0)
    cc = lax.broadcasted_iota(jnp.int32, (SGU_CHUNK, SGU_CHUNK), 1)
    for hh in range(H_B):
        u_h = jax.nn.gelu(proj[:, P_U + hh * D_B:P_U + (hh + 1) * D_B])
        v_h = _rms(jax.nn.gelu(proj[:, P_VB + hh * D_B:P_VB + (hh + 1) * D_B]),
                   sgun_ref[:, hh * D_B:(hh + 1) * D_B])
        w_h = jnp.where(rc >= cc, wsgu_ref[hh], 0.0)
        b_col = _lane_col(bsgu_ref[...], hh)
        z_rows = []
        for s in range(tb // SGU_CHUNK):
            r0 = s * SGU_CHUNK
            z_rows.append(_dot(w_h, v_h[r0:r0 + SGU_CHUNK]) + b_col)
        o_heads.append(u_h * jnp.concatenate(z_rows, axis=0))

    mix = jnp.concatenate(o_heads, axis=1)
    xo_ref[0] = x + g1 * _dot(mix, w_out_ref[...])

    @pl.when(t == nt - 1)
    def _():
        s_out_ref[0] = s_scr[...]


def _prompt_mixer(l, x, mod, norm1, w_in, convw, alog_row, dtb_row, gdn, sgun, wsgu, bsgu_t, w_out):
    nb, seq, _ = x.shape
    tb = TB_MIX
    kern = functools.partial(_mixer_kernel, tb=tb)
    consts = (norm1, w_in, convw, alog_row, dtb_row, gdn, sgun, wsgu, bsgu_t, w_out)
    return pl.pallas_call(
        kern,
        grid=(nb, seq // tb),
        in_specs=[
            pl.BlockSpec((1, tb, D_MODEL), lambda b, t: (b, t, 0)),
            pl.BlockSpec((None, 1, 1, N_MOD * D_MODEL), lambda b, t: (l, b, 0, 0)),
        ] + [_layer_spec(a, l) for a in consts],
        out_specs=[
            pl.BlockSpec((1, tb, D_MODEL), lambda b, t: (b, t, 0)),
            pl.BlockSpec((1, H_A, DK, DV), lambda b, t: (b, 0, 0, 0)),
            pl.BlockSpec((1, CONV_A - 1, QKV_DIM), lambda b, t: (b, 0, 0)),
        ],
        out_shape=[
            jax.ShapeDtypeStruct((nb, seq, D_MODEL), F32),
            jax.ShapeDtypeStruct((nb, H_A, DK, DV), F32),
            jax.ShapeDtypeStruct((nb, CONV_A - 1, QKV_DIM), F32),
        ],
        scratch_shapes=[
            pltpu.VMEM((tb + SUBLANES, QKV_DIM), F32),
            pltpu.VMEM((H_A, DK, DV), F32),
        ],
        compiler_params=pltpu.CompilerParams(
            dimension_semantics=("arbitrary", "arbitrary"), vmem_limit_bytes=VMEM_LIMIT),
        name="prompt_mixer",
    )(x, mod, norm1, w_in, convw, alog_row, dtb_row, gdn, sgun, wsgu, bsgu_t, w_out)


def _ffn_body(x, mod_row, norm2, w_up_ref, upc_fn, w_down_ref):
    sh2 = mod_row[:, 3 * D_MODEL:4 * D_MODEL]
    sc2 = mod_row[:, 4 * D_MODEL:5 * D_MODEL]
    g2 = mod_row[:, 5 * D_MODEL:6 * D_MODEL]
    h2 = _rms(x, norm2) * (1.0 + sc2) + sh2
    up = _dot(h2, w_up_ref[...])
    upc = upc_fn(up)
    gated = jax.nn.silu(upc[:, 0:D_FF]) * upc[:, D_FF:2 * D_FF]
    return x + g2 * _dot(gated, w_down_ref[...])


def _ffn_kernel(x_ref, mod_ref, norm2_ref, w_up_ref, convw_ref, convb_ref, w_down_ref, fnorm_ref,
                xo_ref, ffnbuf_ref, xp_scr, *, tb, final):
    t = pl.program_id(1)

    @pl.when(t == 0)
    def _():
        xp_scr[0:SUBLANES, :] = jnp.zeros((SUBLANES, 2 * D_FF), F32)

    def conv(up):
        xp_scr[SUBLANES:SUBLANES + tb, :] = up
        acc = up * convw_ref[CONV_F - 1:CONV_F, :] + convb_ref[...]
        for j in range(CONV_F - 1):
            off = SUBLANES - (CONV_F - 1) + j
            acc = acc + xp_scr[off:off + tb, :] * convw_ref[j:j + 1, :]
        ffnbuf_ref[0] = xp_scr[tb + SUBLANES - (CONV_F - 1):tb + SUBLANES, :]
        xp_scr[0:SUBLANES, :] = xp_scr[tb:tb + SUBLANES, :]
        return acc

    x_new = _ffn_body(x_ref[0], mod_ref[0], norm2_ref[...], w_up_ref, conv, w_down_ref)
    if final:
        x_new = _rms(x_new, fnorm_ref[...])
    xo_ref[0] = x_new


def _prompt_ffn(l, x, mod, norm2, w_up, convw, convb, w_down, fnorm, final):
    nb, seq, _ = x.shape
    tb = TB_FFN
    kern = functools.partial(_ffn_kernel, tb=tb, final=final)
    return pl.pallas_call(
        kern,
        grid=(nb, seq // tb),
        in_specs=[
            pl.BlockSpec((1, tb, D_MODEL), lambda b, t: (b, t, 0)),
            pl.BlockSpec((None, 1, 1, N_MOD * D_MODEL), lambda b, t: (l, b, 0, 0)),
        ] + [_layer_spec(a, l) for a in (norm2, w_up, convw, convb, w_down)] + [
            _const_spec((1, D_MODEL)),
        ],
        out_specs=[
            pl.BlockSpec((1, tb, D_MODEL), lambda b, t: (b, t, 0)),
            pl.BlockSpec((1, CONV_F - 1, 2 * D_FF), lambda b, t: (b, 0, 0)),
        ],
        out_shape=[
            jax.ShapeDtypeStruct((nb, seq, D_MODEL), F32),
            jax.ShapeDtypeStruct((nb, CONV_F - 1, 2 * D_FF), F32),
        ],
        scratch_shapes=[pltpu.VMEM((tb + SUBLANES, 2 * D_FF), F32)],
        compiler_params=pltpu.CompilerParams(
            dimension_semantics=("arbitrary", "arbitrary"), vmem_limit_bytes=VMEM_LIMIT),
        name="prompt_ffn",
    )(x, mod, norm2, w_up, convw, convb, w_down, fnorm)


def _bcast_heads(cols):
    return jnp.concatenate([jnp.broadcast_to(c, (c.shape[0], LANES)) for c in cols], axis=1)


def _sample_in_kernel(x_ref, mod_ref, norm1_ref, w_in_ref, cstate_ref, convw_ref, alog_ref, dtb_ref,
                      sgun_ref, wd_ref, b0_ref,
                      ke_ref, qe_ref, kn_ref, v_ref, beta_ref, eg_ref, qk_ref, z_ref, ob_ref, vv_ref,
                      cnew_ref):
    x = x_ref[...]
    sh1 = mod_ref[:, 0:D_MODEL]
    sc1 = mod_ref[:, D_MODEL:2 * D_MODEL]
    h = _rms(x, norm1_ref[...]) * (1.0 + sc1) + sh1
    proj = _dot(h, w_in_ref[...])
    qkv = proj[:, 0:QKV_DIM]
    acc = qkv * convw_ref[CONV_A - 1:CONV_A, :]
    for j in range(CONV_A - 1):
        acc = acc + cstate_ref[j] * convw_ref[j:j + 1, :]
    cnew_ref[...] = qkv
    qkvc = jax.nn.silu(acc)

    sm = proj[:, P_SM:P_SM + LANES]
    beta_all = jax.nn.sigmoid(sm)
    eg_all = jnp.exp(_decay_logits(sm, alog_ref[...], dtb_ref[...]))
    beta_cols, eg_cols, qk_cols = [], [], []
    for hh in range(H_A):
        q_h = _l2n(qkvc[:, hh * DK:(hh + 1) * DK]) * (DK ** -0.5)
        k_h = _l2n(qkvc[:, QK_DIM + hh * DK:QK_DIM + (hh + 1) * DK])
        eg_c = _lane_col(eg_all, H_A + hh)
        sl = slice(hh * DK, (hh + 1) * DK)
        ke_ref[:, sl] = k_h * eg_c
        qe_ref[:, sl] = q_h * eg_c
        kn_ref[:, sl] = k_h
        beta_cols.append(_lane_col(beta_all, hh))
        eg_cols.append(eg_c)
        qk_cols.append(jnp.sum(q_h * k_h, axis=1, keepdims=True))
    v_ref[...] = qkvc[:, 2 * QK_DIM:2 * QK_DIM + V_DIM]
    beta_ref[...] = _bcast_heads(beta_cols)
    eg_ref[...] = _bcast_heads(eg_cols)
    qk_ref[...] = _bcast_heads(qk_cols)
    z_ref[...] = proj[:, P_Z:P_Z + V_DIM]

    vv_parts = []
    for hh in range(H_B):
        vv_parts.append(_rms(jax.nn.gelu(proj[:, P_VB + hh * D_B:P_VB + (hh + 1) * D_B]),
                             sgun_ref[:, hh * D_B:(hh + 1) * D_B]))
    vv = jnp.concatenate(vv_parts, axis=1)
    vv_ref[...] = vv
    ob_ref[...] = jax.nn.gelu(proj[:, P_U:P_U + SGU_DIM]) * (wd_ref[...] * vv + b0_ref[...])


def _sample_in(l, x, mod, norm1, w_in, cstate, convw, alog_row, dtb_row, sgun, wd_row, b0_row):
    n = x.shape[0]
    wide = jax.ShapeDtypeStruct((n, V_DIM), F32)
    stacked = (mod, norm1, w_in, cstate, convw, alog_row, dtb_row, sgun, wd_row, b0_row)
    outs = [wide] * 10 + [jax.ShapeDtypeStruct((n, QKV_DIM), F32)]
    return pl.pallas_call(
        _sample_in_kernel,
        grid=(1,),
        in_specs=[_const_spec(x.shape)] + [_layer_spec(a, l) for a in stacked],
        out_specs=[pl.BlockSpec(o.shape, lambda i: (0, 0)) for o in outs],
        out_shape=outs,
        compiler_params=pltpu.CompilerParams(
            dimension_semantics=("arbitrary",), vmem_limit_bytes=VMEM_LIMIT),
        name="sample_in",
    )(x, *stacked)


def _sample_state_kernel(s_ref, ke_ref, qe_ref, kn_ref, v_ref, beta_ref, eg_ref, qk_ref, *rest):
    s_out_ref, o_ref = rest[-2:]
    bb = BB_STATE
    row = lax.broadcasted_iota(jnp.int32, (bb, DK), 0)
    for hh in range(H_A):
        sl = slice(hh * DK, (hh + 1) * DK)
        lhs = jnp.concatenate([ke_ref[:, sl], qe_ref[:, sl]], axis=0)
        pred_k = jnp.zeros((bb, DV), F32)
        pred_q = jnp.zeros((bb, DV), F32)
        for i in range(bb):
            r = jnp.dot(lhs, s_ref[i, hh], preferred_element_type=F32)
            pred_k = pred_k + jnp.where(row == i, r[0:bb], 0.0)
            pred_q = pred_q + jnp.where(row == i, r[bb:2 * bb], 0.0)
        v_new = beta_ref[:, sl] * (v_ref[:, sl] - pred_k)
        o_ref[:, sl] = pred_q + qk_ref[:, sl] * v_new
        kn = kn_ref[:, sl]
        eg = eg_ref[:, sl]
        for i in range(bb):
            upd = lax.dot_general(jnp.where(row == i, kn, 0.0), v_new, (((0,), (0,)), ((), ())),
                                  preferred_element_type=F32)
            s_out_ref[i, hh] = s_ref[i, hh] * eg[i:i + 1, :] + upd


def _sample_state(l, s_all, s_new_all, ke, qe, kn, v, beta, eg, qk):
    n = s_all.shape[1]
    bb = BB_STATE
    row_spec = pl.BlockSpec((bb, V_DIM), lambda i: (i, 0))
    s_spec = pl.BlockSpec((None, bb, H_A, DK, DV), lambda i: (l, i, 0, 0, 0))
    ins = [s_all, ke, qe, kn, v, beta, eg, qk]
    in_specs = [s_spec] + [row_spec] * 7
    aliases = {}
    if s_new_all is not None:
        ins.append(s_new_all)
        in_specs.append(pl.BlockSpec(memory_space=pl.ANY))
        aliases = {len(ins) - 1: 0}
    return pl.pallas_call(
        _sample_state_kernel,
        grid=(n // bb,),
        in_specs=in_specs,
        out_specs=[s_spec, row_spec],
        out_shape=[jax.ShapeDtypeStruct(s_all.shape, F32), jax.ShapeDtypeStruct((n, V_DIM), F32)],
        input_output_aliases=aliases,
        compiler_params=pltpu.CompilerParams(
            dimension_semantics=("arbitrary",), vmem_limit_bytes=VMEM_LIMIT),
        name="sample_state",
    )(*ins)


def _sample_out_kernel(x_ref, mod_ref, o_ref, z_ref, ob_ref, gdn_ref, w_out_ref, norm2_ref, w_up_ref,
                       fstate_ref, convw_ref, convb_ref, w_down_ref, fnorm_ref,
                       xo_ref, fnew_ref, *, final):
    x = x_ref[...]
    g1 = mod_ref[:, 2 * D_MODEL:3 * D_MODEL]
    parts = []
    for hh in range(H_A):
        sl = slice(hh * DV, (hh + 1) * DV)
        parts.append(_rms(o_ref[:, sl], gdn_ref[...]) * jax.nn.silu(z_ref[:, sl]))
    parts.append(ob_ref[...])
    mix = jnp.concatenate(parts, axis=1)
    x = x + g1 * _dot(mix, w_out_ref[...])

    def conv(up):
        acc = up * convw_ref[CONV_F - 1:CONV_F, :] + convb_ref[...]
        for j in range(CONV_F - 1):
            acc = acc + fstate_ref[j] * convw_ref[j:j + 1, :]
        fnew_ref[...] = up
        return acc

    x_new = _ffn_body(x, mod_ref[...], norm2_ref[...], w_up_ref, conv, w_down_ref)
    if final:
        x_new = _rms(x_new, fnorm_ref[...])
    xo_ref[...] = x_new


def _sample_out(l, x, mod, o, z, ob, gdn, w_out, norm2, w_up, fstate, convw, convb, w_down, fnorm,
                final):
    n = x.shape[0]
    ins = (x, mod, o, z, ob, gdn, w_out, norm2, w_up, fstate, convw, convb, w_down, fnorm)
    per_layer = (False, True, False, False, False, True, True, True, True, True, True, True, True, False)
    outs = [jax.ShapeDtypeStruct((n, D_MODEL), F32),
            jax.ShapeDtypeStruct((n, 2 * D_FF), F32)]
    return pl.pallas_call(
        functools.partial(_sample_out_kernel, final=final),
        grid=(1,),
        in_specs=[_layer_spec(a, l) if st else _const_spec(a.shape) for a, st in zip(ins, per_layer)],
        out_specs=[pl.BlockSpec(o_.shape, lambda i: (0, 0)) for o_ in outs],
        out_shape=outs,
        compiler_params=pltpu.CompilerParams(
            dimension_semantics=("arbitrary",), vmem_limit_bytes=VMEM_LIMIT),
        name="sample_out",
    )(*ins)


def _lane_rows(vals, offset):
    return jnp.zeros((vals.shape[0], 1, LANES), F32).at[:, 0, offset:offset + vals.shape[1]].set(vals)


def kernel(x_prompt, x_sample, state_delta, state_qkv_conv, state_ffn_conv, c_prompt, c_sample,
           w_mod, b_mod, norm1, w_in, conv_qkv, a_log, dt_bias, gdn_norm, sgu_norm, w_sgu, b_sgu,
           w_out, norm2, w_up, conv_ffn_w, conv_ffn_b, w_down, final_norm):
    nbp = x_prompt.shape[0]
    nbs = x_sample.shape[0]

    off_z = QKV_DIM
    off_beta = off_z + V_DIM
    off_u = off_beta + 2 * H_A
    off_vb = off_u + SGU_DIM
    w_in_p = jnp.concatenate(
        [w_in[:, :, 0:off_beta], w_in[:, :, off_u:off_vb + SGU_DIM], w_in[:, :, off_beta:off_u],
         jnp.zeros((DEPTH, D_MODEL, LANES - 2 * H_A), w_in.dtype)], axis=2).astype(BF16)
    w_out_b = w_out.astype(BF16)
    w_up_b = w_up.astype(BF16)
    w_down_b = w_down.astype(BF16)

    mod_all = _modulation(jnp.concatenate([c_prompt, c_sample], axis=0), w_mod, b_mod)
    mod_p = mod_all[:, :nbp].reshape(DEPTH, nbp, 1, N_MOD * D_MODEL)
    mod_s = mod_all[:, nbp:]
    fnorm = final_norm.reshape(1, D_MODEL)

    n1 = norm1.reshape(DEPTH, 1, D_MODEL)
    n2 = norm2.reshape(DEPTH, 1, D_MODEL)
    alog_rows = _lane_rows(a_log, H_A)
    dtb_rows = _lane_rows(dt_bias, H_A)
    gdn = gdn_norm.reshape(DEPTH, 1, DV)
    sgun = sgu_norm.reshape(DEPTH, 1, SGU_DIM)
    convb = conv_ffn_b.reshape(DEPTH, 1, 2 * D_FF)
    bsgu_t = jnp.zeros((DEPTH, SGU_CHUNK, LANES), F32).at[:, :, 0:H_B].set(
        jnp.swapaxes(b_sgu, 1, 2))
    wd_rows = jnp.repeat(w_sgu[:, :, 0, 0], D_B, axis=1).reshape(DEPTH, 1, SGU_DIM)
    b0_rows = jnp.repeat(b_sgu[:, :, 0], D_B, axis=1).reshape(DEPTH, 1, SGU_DIM)

    xp = x_prompt
    xs = x_sample.reshape(nbs, D_MODEL)
    cstate = jnp.swapaxes(state_qkv_conv, 1, 2)
    fstate = jnp.swapaxes(state_ffn_conv, 1, 2)
    delta_s = None
    delta_p, qkv_p, qkv_s, ffn_p, ffn_s, vv_s = [], [], [], [], [], []
    for l in range(DEPTH):
        final = l == DEPTH - 1
        xp, s_p, qb_p = _prompt_mixer(l, xp, mod_p, n1, w_in_p, conv_qkv, alog_rows, dtb_rows,
                                      gdn, sgun, w_sgu, bsgu_t, w_out_b)
        xp, fb_p = _prompt_ffn(l, xp, mod_p, n2, w_up_b, conv_ffn_w, convb, w_down_b, fnorm, final)
        delta_p.append(s_p)
        qkv_p.append(qb_p)
        ffn_p.append(fb_p)

        (ke, qe, kn, v, beta, eg, qk, z, ob, vv, c_new) = _sample_in(
            l, xs, mod_s, n1, w_in_p, cstate, conv_qkv, alog_rows, dtb_rows, sgun, wd_rows, b0_rows)
        delta_s, o = _sample_state(l, state_delta, delta_s, ke, qe, kn, v, beta, eg, qk)
        xs, f_new = _sample_out(l, xs, mod_s, o, z, ob, gdn, w_out_b, n2, w_up_b, fstate,
                                conv_ffn_w, convb, w_down_b, fnorm, final)
        qkv_s.append(c_new)
        ffn_s.append(f_new)
        vv_s.append(vv.reshape(nbs, 1, H_B, D_B))

    qkv_s_all = jnp.concatenate([state_qkv_conv[:, :, 1:, :], jnp.stack(qkv_s)[:, :, None, :]], axis=2)
    ffn_s_all = jnp.concatenate([state_ffn_conv[:, :, 1:, :], jnp.stack(ffn_s)[:, :, None, :]], axis=2)
    return (xp, xs.reshape(nbs, 1, D_MODEL), jnp.stack(delta_p), delta_s,
            jnp.stack(qkv_p), qkv_s_all, jnp.stack(ffn_p), ffn_s_all,
            jnp.stack(vv_s))
```

```python
import functools
import math

import jax
import jax.numpy as jnp
from jax import lax
from jax.experimental import pallas as pl
from jax.experimental.pallas import tpu as pltpu

F32 = jnp.float32
BF16 = jnp.bfloat16

D_MODEL = 1024
DEPTH = 4
H_A = 4
DK = 128
DV = 128
QK_DIM = H_A * DK
V_DIM = H_A * DV
QKV_DIM = 2 * QK_DIM + V_DIM
CONV_A = 4
H_B = 4
D_B = 128
SGU_DIM = H_B * D_B
SGU_CHUNK = 128
D_FF = 2816
CONV_F = 3
N_MOD = 6
EPS = 1e-6

P_Z = QKV_DIM
P_U = P_Z + V_DIM
P_VB = P_U + SGU_DIM
P_SM = P_VB + SGU_DIM
P_DIM = P_SM + 128

LANES = 128
SUBLANES = 8
SUPER = 128
DELTA_C = 64
TB_MIX = 512
TB_FFN = 256
BB_STATE = 8
VMEM_LIMIT = 56 * 1024 * 1024


def _dot(a, b):
    return jnp.dot(a.astype(BF16), b.astype(BF16), preferred_element_type=F32)


def _dot_nt(a, b):
    return lax.dot_general(a.astype(BF16), b.astype(BF16), (((1,), (1,)), ((), ())),
                           preferred_element_type=F32)


def _dot_tn(a, b):
    return lax.dot_general(a.astype(BF16), b.astype(BF16), (((0,), (0,)), ((), ())),
                           preferred_element_type=F32)


def _rms(x, w):
    return x * lax.rsqrt(jnp.mean(x * x, axis=-1, keepdims=True) + EPS) * w


def _l2n(x):
    return x * lax.rsqrt(jnp.sum(x * x, axis=-1, keepdims=True) + EPS)


def _lane_col(x, lane):
    ids = lax.broadcasted_iota(jnp.int32, x.shape, 1)
    return jnp.sum(jnp.where(ids == lane, x, 0.0), axis=1, keepdims=True)


def _decay_logits(sm, alog_row, dtb_row):
    return -jnp.exp(alog_row) * jax.nn.softplus(sm + dtb_row)


def _const_spec(shape):
    nd = len(shape)
    return pl.BlockSpec(shape, lambda *_: (0,) * nd, pipeline_mode=pl.Buffered(1))


def _layer_spec(arr, l):
    shape = tuple(arr.shape[1:])
    return pl.BlockSpec((None,) + shape, lambda *_: (l,) + (0,) * len(shape),
                        pipeline_mode=pl.Buffered(1))


def _mod_kernel(c_ref, w_ref, b_ref, o_ref):
    c_act = jax.nn.silu(c_ref[...])
    o_ref[0] = _dot(c_act, w_ref[0]) + b_ref[0]


def _modulation(c_all, w_mod, b_mod):
    n = c_all.shape[0]
    tn = 1536
    return pl.pallas_call(
        _mod_kernel,
        grid=(DEPTH, (N_MOD * D_MODEL) // tn),
        in_specs=[
            pl.BlockSpec((n, D_MODEL), lambda l, j: (0, 0)),
            pl.BlockSpec((1, D_MODEL, tn), lambda l, j: (l, 0, j)),
            pl.BlockSpec((1, 1, tn), lambda l, j: (l, 0, j)),
        ],
        out_specs=pl.BlockSpec((1, n, tn), lambda l, j: (l, 0, j)),
        out_shape=jax.ShapeDtypeStruct((DEPTH, n, N_MOD * D_MODEL), F32),
        compiler_params=pltpu.CompilerParams(
            dimension_semantics=("arbitrary", "arbitrary"), vmem_limit_bytes=VMEM_LIMIT),
        name="modulation",
    )(c_all, w_mod, b_mod.reshape(DEPTH, 1, N_MOD * D_MODEL))


def _mixer_kernel(x_ref, mod_ref, norm1_ref, w_in_ref, convw_ref, alog_ref, dtb_ref,
                  gdn_ref, sgun_ref, wsgu_ref, bsgu_ref, w_out_ref,
                  xo_ref, s_out_ref, qkvbuf_ref,
                  xp_scr, s_scr, *, tb):
    t = pl.program_id(1)
    nt = pl.num_programs(1)

    @pl.when(t == 0)
    def _():
        xp_scr[0:SUBLANES, :] = jnp.zeros((SUBLANES, QKV_DIM), F32)
        s_scr[...] = jnp.zeros(s_scr.shape, F32)

    x = x_ref[0]
    sh1 = mod_ref[0, :, 0:D_MODEL]
    sc1 = mod_ref[0, :, D_MODEL:2 * D_MODEL]
    g1 = mod_ref[0, :, 2 * D_MODEL:3 * D_MODEL]
    h = _rms(x, norm1_ref[...]) * (1.0 + sc1) + sh1
    proj = _dot(h, w_in_ref[...])

    qkv = proj[:, 0:QKV_DIM]
    xp_scr[SUBLANES:SUBLANES + tb, :] = qkv
    acc = qkv * convw_ref[3:4, :]
    for j in range(CONV_A - 1):
        off = SUBLANES - (CONV_A - 1) + j
        acc = acc + xp_scr[off:off + tb, :] * convw_ref[j:j + 1, :]
    qkvbuf_ref[0] = xp_scr[tb + SUBLANES - (CONV_A - 1):tb + SUBLANES, :]
    xp_scr[0:SUBLANES, :] = xp_scr[tb:tb + SUBLANES, :]
    qkvc = jax.nn.silu(acc)

    sm = proj[:, P_SM:P_SM + LANES]
    beta_all = jax.nn.sigmoid(sm)
    g_all = _decay_logits(sm, alog_ref[...], dtb_ref[...])

    ri = lax.broadcasted_iota(jnp.int32, (tb, tb), 0)
    ci = lax.broadcasted_iota(jnp.int32, (tb, tb), 1)
    cum_mat = jnp.where((ri // DELTA_C == ci // DELTA_C) & (ri >= ci), 1.0, 0.0).astype(BF16)
    g_hi = g_all.astype(BF16)
    r1 = g_all - g_hi.astype(F32)
    g_mid = r1.astype(BF16)
    g_lo = (r1 - g_mid.astype(F32)).astype(BF16)
    gc_all = (jnp.dot(cum_mat, g_hi, preferred_element_type=F32)
              + jnp.dot(cum_mat, g_mid, preferred_element_type=F32)
              + jnp.dot(cum_mat, g_lo, preferred_element_type=F32))
    gc_t = gc_all.T

    rs = lax.broadcasted_iota(jnp.int32, (SUPER, SUPER), 0)
    cs = lax.broadcasted_iota(jnp.int32, (SUPER, SUPER), 1)
    same_chunk = (rs // DELTA_C) == (cs // DELTA_C)
    tril = same_chunk & (rs >= cs)
    strict = same_chunk & (rs > cs)
    eye = jnp.where(rs == cs, 1.0, 0.0)
    pair_mask = (rs // 2 == cs // 2) & (rs > cs)
    quad_masks = []
    half = 2
    while half < DELTA_C:
        quad_masks.append((rs // (2 * half) == cs // (2 * half))
                          & (rs % (2 * half) >= half) & (cs % (2 * half) < half))
        half *= 2
    chunks_per_super = SUPER // DELTA_C

    row_chunk = lax.broadcasted_iota(jnp.int32, (SUPER, 1), 0) // DELTA_C
    n_super = tb // SUPER

    pairs = [(hh, s) for s in range(n_super) for hh in range(H_A)]
    head_in = []
    for hh in range(H_A):
        head_in.append((
            _l2n(qkvc[:, hh * DK:(hh + 1) * DK]) * (DK ** -0.5),
            _l2n(qkvc[:, QK_DIM + hh * DK:QK_DIM + (hh + 1) * DK]),
            qkvc[:, 2 * QK_DIM + hh * DV:2 * QK_DIM + (hh + 1) * DV],
            _lane_col(beta_all, hh),
            _lane_col(gc_all, H_A + hh)))

    low_p, qk_p, rhs_p, qg_p, kdec_p, glast_p = {}, {}, {}, {}, {}, {}
    for p in pairs:
        hh, s = p
        q_h, k_h, v_h, beta_c, gc_c = head_in[hh]
        r0 = s * SUPER
        qs = q_h[r0:r0 + SUPER]
        ks = k_h[r0:r0 + SUPER]
        bs = beta_c[r0:r0 + SUPER]
        gcol = gc_c[r0:r0 + SUPER]
        grow = gc_t[H_A + hh:H_A + hh + 1, r0:r0 + SUPER]
        decay = jnp.exp(jnp.where(tril, gcol - grow, -jnp.inf))
        kb = ks * bs
        eg = jnp.exp(gcol)
        low_p[p] = jnp.where(strict, _dot_nt(kb, ks) * decay, 0.0)
        qk_p[p] = jnp.where(tril, _dot_nt(qs, ks) * decay, 0.0)
        rhs_p[p] = jnp.concatenate([v_h[r0:r0 + SUPER] * bs, kb * eg], axis=1)
        qg_p[p] = qs * eg
        g_last = [grow[:, (c + 1) * DELTA_C - 1:(c + 1) * DELTA_C] for c in range(chunks_per_super)]
        glast_col = g_last[-1]
        for c in range(chunks_per_super - 2, -1, -1):
            glast_col = jnp.where(row_chunk == c, g_last[c], glast_col)
        kdec_p[p] = ks * jnp.exp(glast_col - gcol)
        glast_p[p] = g_last

    inv_p = {p: eye - jnp.where(pair_mask, low_p[p], 0.0) for p in pairs}
    for qm in quad_masks:
        t_p = {p: _dot(inv_p[p], jnp.where(qm, low_p[p], 0.0)) for p in pairs}
        inv_p = {p: inv_p[p] - _dot(t_p[p], inv_p[p]) for p in pairs}
    sol_p = {p: _dot(inv_p[p], rhs_p[p]) for p in pairs}
    qs_p = {p: _dot(qk_p[p], sol_p[p]) for p in pairs}
    bm_p = {(p, c): _dot_tn(kdec_p[p][c * DELTA_C:(c + 1) * DELTA_C],
                            sol_p[p][c * DELTA_C:(c + 1) * DELTA_C])
            for p in pairs for c in range(chunks_per_super)}

    state = [s_scr[hh] for hh in range(H_A)]
    o_rows = [[] for _ in range(H_A)]
    for s in range(n_super):
        for c in range(chunks_per_super):
            c0 = c * DELTA_C
            for hh in range(H_A):
                p = (hh, s)
                s_b = state[hh].astype(BF16)
                q_eff = qg_p[p][c0:c0 + DELTA_C] - qs_p[p][c0:c0 + DELTA_C, DV:DV + DK]
                o_rows[hh].append(_dot(q_eff, s_b) + qs_p[p][c0:c0 + DELTA_C, 0:DV])
                bm = bm_p[(p, c)]
                state[hh] = (state[hh] * jnp.exp(glast_p[p][c]) - _dot(bm[:, DV:DV + DK], s_b)
                             + bm[:, 0:DV])
    o_heads = []
    for hh in range(H_A):
        s_scr[hh] = state[hh]
        o_h = jnp.concatenate(o_rows[hh], axis=0)
        z_h = proj[:, P_Z + hh * DV:P_Z + (hh + 1) * DV]
        o_heads.append(_rms(o_h, gdn_ref[...]) * jax.nn.silu(z_h))

    rc = lax.broadcasted_iota(jnp.int32, (SGU_CHUNK, SGU_CHUNK), 0)
    cc = lax.broadcasted_iota(jnp.int32, (SGU_CHUNK, SGU_CHUNK), 1)
    for hh in range(H_B):
        u_h = jax.nn.gelu(proj[:, P_U + hh * D_B:P_U + (hh + 1) * D_B])
        v_h = _rms(jax.nn.gelu(proj[:, P_VB + hh * D_B:P_VB + (hh + 1) * D_B]),
                   sgun_ref[:, hh * D_B:(hh + 1) * D_B])
        w_h = jnp.where(rc >= cc, wsgu_ref[hh], 0.0)
        b_col = _lane_col(bsgu_ref[...], hh)
        z_rows = []
        for s in range(tb // SGU_CHUNK):
            r0 = s * SGU_CHUNK
            z_rows.append(_dot(w_h, v_h[r0:r0 + SGU_CHUNK]) + b_col)
        o_heads.append(u_h * jnp.concatenate(z_rows, axis=0))

    mix = jnp.concatenate(o_heads, axis=1)
    xo_ref[0] = x + g1 * _dot(mix, w_out_ref[...])

    @pl.when(t == nt - 1)
    def _():
        s_out_ref[0] = s_scr[...]


def _prompt_mixer(l, x, mod, norm1, w_in, convw, alog_row, dtb_row, gdn, sgun, wsgu, bsgu_t, w_out):
    nb, seq, _ = x.shape
    tb = TB_MIX
    kern = functools.partial(_mixer_kernel, tb=tb)
    consts = (norm1, w_in, convw, alog_row, dtb_row, gdn, sgun, wsgu, bsgu_t, w_out)
    return pl.pallas_call(
        kern,
        grid=(nb, seq // tb),
        in_specs=[
            pl.BlockSpec((1, tb, D_MODEL), lambda b, t: (b, t, 0)),
            pl.BlockSpec((None, 1, 1, N_MOD * D_MODEL), lambda b, t: (l, b, 0, 0)),
        ] + [_layer_spec(a, l) for a in consts],
        out_specs=[
            pl.BlockSpec((1, tb, D_MODEL), lambda b, t: (b, t, 0)),
            pl.BlockSpec((1, H_A, DK, DV), lambda b, t: (b, 0, 0, 0)),
            pl.BlockSpec((1, CONV_A - 1, QKV_DIM), lambda b, t: (b, 0, 0)),
        ],
        out_shape=[
            jax.ShapeDtypeStruct((nb, seq, D_MODEL), F32),
            jax.ShapeDtypeStruct((nb, H_A, DK, DV), F32),
            jax.ShapeDtypeStruct((nb, CONV_A - 1, QKV_DIM), F32),
        ],
        scratch_shapes=[
            pltpu.VMEM((tb + SUBLANES, QKV_DIM), F32),
            pltpu.VMEM((H_A, DK, DV), F32),
        ],
        compiler_params=pltpu.CompilerParams(
            dimension_semantics=("arbitrary", "arbitrary"), vmem_limit_bytes=VMEM_LIMIT),
        name="prompt_mixer",
    )(x, mod, norm1, w_in, convw, alog_row, dtb_row, gdn, sgun, wsgu, bsgu_t, w_out)


def _ffn_body(x, mod_row, norm2, w_up_ref, upc_fn, w_down_ref):
    sh2 = mod_row[:, 3 * D_MODEL:4 * D_MODEL]
    sc2 = mod_row[:, 4 * D_MODEL:5 * D_MODEL]
    g2 = mod_row[:, 5 * D_MODEL:6 * D_MODEL]
    h2 = _rms(x, norm2) * (1.0 + sc2) + sh2
    up = _dot(h2, w_up_ref[...])
    upc = upc_fn(up)
    gated = jax.nn.silu(upc[:, 0:D_FF]) * upc[:, D_FF:2 * D_FF]
    return x + g2 * _dot(gated, w_down_ref[...])


def _ffn_kernel(x_ref, mod_ref, norm2_ref, w_up_ref, convw_ref, convb_ref, w_down_ref, fnorm_ref,
                xo_ref, ffnbuf_ref, xp_scr, *, tb, final):
    t = pl.program_id(1)

    @pl.when(t == 0)
    def _():
        xp_scr[0:SUBLANES, :] = jnp.zeros((SUBLANES, 2 * D_FF), F32)

    def conv(up):
        xp_scr[SUBLANES:SUBLANES + tb, :] = up
        acc = up * convw_ref[CONV_F - 1:CONV_F, :] + convb_ref[...]
        for j in range(CONV_F - 1):
            off = SUBLANES - (CONV_F - 1) + j
            acc = acc + xp_scr[off:off + tb, :] * convw_ref[j:j + 1, :]
        ffnbuf_ref[0] = xp_scr[tb + SUBLANES - (CONV_F - 1):tb + SUBLANES, :]
        xp_scr[0:SUBLANES, :] = xp_scr[tb:tb + SUBLANES, :]
        return acc

    x_new = _ffn_body(x_ref[0], mod_ref[0], norm2_ref[...], w_up_ref, conv, w_down_ref)
    if final:
        x_new = _rms(x_new, fnorm_ref[...])
    xo_ref[0] = x_new


def _prompt_ffn(l, x, mod, norm2, w_up, convw, convb, w_down, fnorm, final):
    nb, seq, _ = x.shape
    tb = TB_FFN
    kern = functools.partial(_ffn_kernel, tb=tb, final=final)
    return pl.pallas_call(
        kern,
        grid=(nb, seq // tb),
        in_specs=[
            pl.BlockSpec((1, tb, D_MODEL), lambda b, t: (b, t, 0)),
            pl.BlockSpec((None, 1, 1, N_MOD * D_MODEL), lambda b, t: (l, b, 0, 0)),
        ] + [_layer_spec(a, l) for a in (norm2, w_up, convw, convb, w_down)] + [
            _const_spec((1, D_MODEL)),
        ],
        out_specs=[
            pl.BlockSpec((1, tb, D_MODEL), lambda b, t: (b, t, 0)),
            pl.BlockSpec((1, CONV_F - 1, 2 * D_FF), lambda b, t: (b, 0, 0)),
        ],
        out_shape=[
            jax.ShapeDtypeStruct((nb, seq, D_MODEL), F32),
            jax.ShapeDtypeStruct((nb, CONV_F - 1, 2 * D_FF), F32),
        ],
        scratch_shapes=[pltpu.VMEM((tb + SUBLANES, 2 * D_FF), F32)],
        compiler_params=pltpu.CompilerParams(
            dimension_semantics=("arbitrary", "arbitrary"), vmem_limit_bytes=VMEM_LIMIT),
        name="prompt_ffn",
    )(x, mod, norm2, w_up, convw, convb, w_down, fnorm)


def _bcast_heads(cols):
    return jnp.concatenate([jnp.broadcast_to(c, (c.shape[0], LANES)) for c in cols], axis=1)


def _sample_in_kernel(x_ref, mod_ref, norm1_ref, w_in_ref, cstate_ref, convw_ref, alog_ref, dtb_ref,
                      sgun_ref, wd_ref, b0_ref,
                      ke_ref, qe_ref, kn_ref, v_ref, beta_ref, eg_ref, qk_ref, z_ref, ob_ref, vv_ref,
                      cnew_ref):
    x = x_ref[...]
    sh1 = mod_ref[:, 0:D_MODEL]
    sc1 = mod_ref[:, D_MODEL:2 * D_MODEL]
    h = _rms(x, norm1_ref[...]) * (1.0 + sc1) + sh1
    proj = _dot(h, w_in_ref[...])
    qkv = proj[:, 0:QKV_DIM]
    acc = qkv * convw_ref[CONV_A - 1:CONV_A, :]
    for j in range(CONV_A - 1):
        acc = acc + cstate_ref[j] * convw_ref[j:j + 1, :]
    cnew_ref[...] = qkv
    qkvc = jax.nn.silu(acc)

    sm = proj[:, P_SM:P_SM + LANES]
    beta_all = jax.nn.sigmoid(sm)
    eg_all = jnp.exp(_decay_logits(sm, alog_ref[...], dtb_ref[...]))
    beta_cols, eg_cols, qk_cols = [], [], []
    for hh in range(H_A):
        q_h = _l2n(qkvc[:, hh * DK:(hh + 1) * DK]) * (DK ** -0.5)
        k_h = _l2n(qkvc[:, QK_DIM + hh * DK:QK_DIM + (hh + 1) * DK])
        eg_c = _lane_col(eg_all, H_A + hh)
        sl = slice(hh * DK, (hh + 1) * DK)
        ke_ref[:, sl] = k_h * eg_c
        qe_ref[:, sl] = q_h * eg_c
        kn_ref[:, sl] = k_h
        beta_cols.append(_lane_col(beta_all, hh))
        eg_cols.append(eg_c)
        qk_cols.append(jnp.sum(q_h * k_h, axis=1, keepdims=True))
    v_ref[...] = qkvc[:, 2 * QK_DIM:2 * QK_DIM + V_DIM]
    beta_ref[...] = _bcast_heads(beta_cols)
    eg_ref[...] = _bcast_heads(eg_cols)
    qk_ref[...] = _bcast_heads(qk_cols)
    z_ref[...] = proj[:, P_Z:P_Z + V_DIM]

    vv_parts = []
    for hh in range(H_B):
        vv_parts.append(_rms(jax.nn.gelu(proj[:, P_VB + hh * D_B:P_VB + (hh + 1) * D_B]),
                             sgun_ref[:, hh * D_B:(hh + 1) * D_B]))
    vv = jnp.concatenate(vv_parts, axis=1)
    vv_ref[...] = vv
    ob_ref[...] = jax.nn.gelu(proj[:, P_U:P_U + SGU_DIM]) * (wd_ref[...] * vv + b0_ref[...])


def _sample_in(l, x, mod, norm1, w_in, cstate, convw, alog_row, dtb_row, sgun, wd_row, b0_row):
    n = x.shape[0]
    wide = jax.ShapeDtypeStruct((n, V_DIM), F32)
    stacked = (mod, norm1, w_in, cstate, convw, alog_row, dtb_row, sgun, wd_row, b0_row)
    outs = [wide] * 10 + [jax.ShapeDtypeStruct((n, QKV_DIM), F32)]
    return pl.pallas_call(
        _sample_in_kernel,
        grid=(1,),
        in_specs=[_const_spec(x.shape)] + [_layer_spec(a, l) for a in stacked],
        out_specs=[pl.BlockSpec(o.shape, lambda i: (0, 0)) for o in outs],
        out_shape=outs,
        compiler_params=pltpu.CompilerParams(
            dimension_semantics=("arbitrary",), vmem_limit_bytes=VMEM_LIMIT),
        name="sample_in",
    )(x, *stacked)


def _sample_state_kernel(s_ref, ke_ref, qe_ref, kn_ref, v_ref, beta_ref, eg_ref, qk_ref, *rest):
    s_out_ref, o_ref = rest[-2:]
    bb = BB_STATE
    row = lax.broadcasted_iota(jnp.int32, (bb, DK), 0)
    for hh in range(H_A):
        sl = slice(hh * DK, (hh + 1) * DK)
        lhs = jnp.concatenate([ke_ref[:, sl], qe_ref[:, sl]], axis=0)
        pred_k = jnp.zeros((bb, DV), F32)
        pred_q = jnp.zeros((bb, DV), F32)
        for i in range(bb):
            r = jnp.dot(lhs, s_ref[i, hh], preferred_element_type=F32)
            pred_k = pred_k + jnp.where(row == i, r[0:bb], 0.0)
            pred_q = pred_q + jnp.where(row == i, r[bb:2 * bb], 0.0)
        v_new = beta_ref[:, sl] * (v_ref[:, sl] - pred_k)
        o_ref[:, sl] = pred_q + qk_ref[:, sl] * v_new
        kn = kn_ref[:, sl]
        eg = eg_ref[:, sl]
        for i in range(bb):
            upd = lax.dot_general(jnp.where(row == i, kn, 0.0), v_new, (((0,), (0,)), ((), ())),
                                  preferred_element_type=F32)
            s_out_ref[i, hh] = s_ref[i, hh] * eg[i:i + 1, :] + upd


def _sample_state(l, s_all, s_new_all, ke, qe, kn, v, beta, eg, qk):
    n = s_all.shape[1]
    bb = BB_STATE
    row_spec = pl.BlockSpec((bb, V_DIM), lambda i: (i, 0))
    s_spec = pl.BlockSpec((None, bb, H_A, DK, DV), lambda i: (l, i, 0, 0, 0))
    ins = [s_all, ke, qe, kn, v, beta, eg, qk]
    in_specs = [s_spec] + [row_spec] * 7
    aliases = {}
    if s_new_all is not None:
        ins.append(s_new_all)
        in_specs.append(pl.BlockSpec(memory_space=pl.ANY))
        aliases = {len(ins) - 1: 0}
    return pl.pallas_call(
        _sample_state_kernel,
        grid=(n // bb,),
        in_specs=in_specs,
        out_specs=[s_spec, row_spec],
        out_shape=[jax.ShapeDtypeStruct(s_all.shape, F32), jax.ShapeDtypeStruct((n, V_DIM), F32)],
        input_output_aliases=aliases,
        compiler_params=pltpu.CompilerParams(
            dimension_semantics=("arbitrary",), vmem_limit_bytes=VMEM_LIMIT),
        name="sample_state",
    )(*ins)


def _sample_out_kernel(x_ref, mod_ref, o_ref, z_ref, ob_ref, gdn_ref, w_out_ref, norm2_ref, w_up_ref,
                       fstate_ref, convw_ref, convb_ref, w_down_ref, fnorm_ref,
                       xo_ref, fnew_ref, *, final):
    x = x_ref[...]
    g1 = mod_ref[:, 2 * D_MODEL:3 * D_MODEL]
    parts = []
    for hh in range(H_A):
        sl = slice(hh * DV, (hh + 1) * DV)
        parts.append(_rms(o_ref[:, sl], gdn_ref[...]) * jax.nn.silu(z_ref[:, sl]))
    parts.append(ob_ref[...])
    mix = jnp.concatenate(parts, axis=1)
    x = x + g1 * _dot(mix, w_out_ref[...])

    def conv(up):
        acc = up * convw_ref[CONV_F - 1:CONV_F, :] + convb_ref[...]
        for j in range(CONV_F - 1):
            acc = acc + fstate_ref[j] * convw_ref[j:j + 1, :]
        fnew_ref[...] = up
        return acc

    x_new = _ffn_body(x, mod_ref[...], norm2_ref[...], w_up_ref, conv, w_down_ref)
    if final:
        x_new = _rms(x_new, fnorm_ref[...])
    xo_ref[...] = x_new


def _sample_out(l, x, mod, o, z, ob, gdn, w_out, norm2, w_up, fstate, convw, convb, w_down, fnorm,
                final):
    n = x.shape[0]
    ins = (x, mod, o, z, ob, gdn, w_out, norm2, w_up, fstate, convw, convb, w_down, fnorm)
    per_layer = (False, True, False, False, False, True, True, True, True, True, True, True, True, False)
    outs = [jax.ShapeDtypeStruct((n, D_MODEL), F32),
            jax.ShapeDtypeStruct((n, 2 * D_FF), F32)]
    return pl.pallas_call(
        functools.partial(_sample_out_kernel, final=final),
        grid=(1,),
        in_specs=[_layer_spec(a, l) if st else _const_spec(a.shape) for a, st in zip(ins, per_layer)],
        out_specs=[pl.BlockSpec(o_.shape, lambda i: (0, 0)) for o_ in outs],
        out_shape=outs,
        compiler_params=pltpu.CompilerParams(
            dimension_semantics=("arbitrary",), vmem_limit_bytes=VMEM_LIMIT),
        name="sample_out",
    )(*ins)


def _lane_rows(vals, offset):
    return jnp.zeros((vals.shape[0], 1, LANES), F32).at[:, 0, offset:offset + vals.shape[1]].set(vals)


def kernel(x_prompt, x_sample, state_delta, state_qkv_conv, state_ffn_conv, c_prompt, c_sample,
           w_mod, b_mod, norm1, w_in, conv_qkv, a_log, dt_bias, gdn_norm, sgu_norm, w_sgu, b_sgu,
           w_out, norm2, w_up, conv_ffn_w, conv_ffn_b, w_down, final_norm):
    nbp = x_prompt.shape[0]
    nbs = x_sample.shape[0]

    off_z = QKV_DIM
    off_beta = off_z + V_DIM
    off_u = off_beta + 2 * H_A
    off_vb = off_u + SGU_DIM
    w_in_p = jnp.concatenate(
        [w_in[:, :, 0:off_beta], w_in[:, :, off_u:off_vb + SGU_DIM], w_in[:, :, off_beta:off_u],
         jnp.zeros((DEPTH, D_MODEL, LANES - 2 * H_A), w_in.dtype)], axis=2).astype(BF16)
    w_out_b = w_out.astype(BF16)
    w_up_b = w_up.astype(BF16)
    w_down_b = w_down.astype(BF16)

    mod_all = _modulation(jnp.concatenate([c_prompt, c_sample], axis=0), w_mod, b_mod)
    mod_p = mod_all[:, :nbp].reshape(DEPTH, nbp, 1, N_MOD * D_MODEL)
    mod_s = mod_all[:, nbp:]
    fnorm = final_norm.reshape(1, D_MODEL)

    n1 = norm1.reshape(DEPTH, 1, D_MODEL)
    n2 = norm2.reshape(DEPTH, 1, D_MODEL)
    alog_rows = _lane_rows(a_log, H_A)
    dtb_rows = _lane_rows(dt_bias, H_A)
    gdn = gdn_norm.reshape(DEPTH, 1, DV)
    sgun = sgu_norm.reshape(DEPTH, 1, SGU_DIM)
    convb = conv_ffn_b.reshape(DEPTH, 1, 2 * D_FF)
    bsgu_t = jnp.zeros((DEPTH, SGU_CHUNK, LANES), F32).at[:, :, 0:H_B].set(
        jnp.swapaxes(b_sgu, 1, 2))
    wd_rows = jnp.repeat(w_sgu[:, :, 0, 0], D_B, axis=1).reshape(DEPTH, 1, SGU_DIM)
    b0_rows = jnp.repeat(b_sgu[:, :, 0], D_B, axis=1).reshape(DEPTH, 1, SGU_DIM)

    xp = x_prompt
    xs = x_sample.reshape(nbs, D_MODEL)
    cstate = jnp.swapaxes(state_qkv_conv, 1, 2)
    fstate = jnp.swapaxes(state_ffn_conv, 1, 2)
    delta_s = jnp.zeros_like(state_delta)
    delta_p, qkv_p, qkv_s, ffn_p, ffn_s, vv_s = [], [], [], [], [], []
    for l in range(DEPTH):
        final = l == DEPTH - 1
        xp, s_p, qb_p = _prompt_mixer(l, xp, mod_p, n1, w_in_p, conv_qkv, alog_rows, dtb_rows,
                                      gdn, sgun, w_sgu, bsgu_t, w_out_b)
        xp, fb_p = _prompt_ffn(l, xp, mod_p, n2, w_up_b, conv_ffn_w, convb, w_down_b, fnorm, final)
        delta_p.append(s_p)
        qkv_p.append(qb_p)
        ffn_p.append(fb_p)

        (ke, qe, kn, v, beta, eg, qk, z, ob, vv, c_new) = _sample_in(
            l, xs, mod_s, n1, w_in_p, cstate, conv_qkv, alog_rows, dtb_rows, sgun, wd_rows, b0_rows)
        delta_s, o = _sample_state(l, state_delta, delta_s, ke, qe, kn, v, beta, eg, qk)
        xs, f_new = _sample_out(l, xs, mod_s, o, z, ob, gdn, w_out_b, n2, w_up_b, fstate,
                                conv_ffn_w, convb, w_down_b, fnorm, final)
        qkv_s.append(c_new)
        ffn_s.append(f_new)
        vv_s.append(vv.reshape(nbs, 1, H_B, D_B))

    qkv_s_all = jnp.concatenate([state_qkv_conv[:, :, 1:, :], jnp.stack(qkv_s)[:, :, None, :]], axis=2)
    ffn_s_all = jnp.concatenate([state_ffn_conv[:, :, 1:, :], jnp.stack(ffn_s)[:, :, None, :]], axis=2)
    return (xp, xs.reshape(nbs, 1, D_MODEL), jnp.stack(delta_p), delta_s,
            jnp.stack(qkv_p), qkv_s_all, jnp.stack(ffn_p), ffn_s_all,
            jnp.stack(vv_s))
```

```python
import functools
import math

import jax
import jax.numpy as jnp
from jax import lax
from jax.experimental import pallas as pl
from jax.experimental.pallas import tpu as pltpu

F32 = jnp.float32
BF16 = jnp.bfloat16

D_MODEL = 1024
DEPTH = 4
H_A = 4
DK = 128
DV = 128
QK_DIM = H_A * DK
V_DIM = H_A * DV
QKV_DIM = 2 * QK_DIM + V_DIM
CONV_A = 4
H_B = 4
D_B = 128
SGU_DIM = H_B * D_B
SGU_CHUNK = 128
D_FF = 2816
CONV_F = 3
N_MOD = 6
EPS = 1e-6

P_Z = QKV_DIM
P_U = P_Z + V_DIM
P_VB = P_U + SGU_DIM
P_SM = P_VB + SGU_DIM
P_DIM = P_SM + 128

LANES = 128
SUBLANES = 8
SUPER = 128
DELTA_C = 64
TB_MIX = 512
TB_FFN = 256
BB_STATE = 8
VMEM_LIMIT = 56 * 1024 * 1024


def _dot(a, b):
    return jnp.dot(a.astype(BF16), b.astype(BF16), preferred_element_type=F32)


def _dot_nt(a, b):
    return lax.dot_general(a.astype(BF16), b.astype(BF16), (((1,), (1,)), ((), ())),
                           preferred_element_type=F32)


def _dot_tn(a, b):
    return lax.dot_general(a.astype(BF16), b.astype(BF16), (((0,), (0,)), ((), ())),
                           preferred_element_type=F32)


def _rms(x, w):
    return x * lax.rsqrt(jnp.mean(x * x, axis=-1, keepdims=True) + EPS) * w


def _l2n(x):
    return x * lax.rsqrt(jnp.sum(x * x, axis=-1, keepdims=True) + EPS)


def _lane_col(x, lane):
    ids = lax.broadcasted_iota(jnp.int32, x.shape, 1)
    return jnp.sum(jnp.where(ids == lane, x, 0.0), axis=1, keepdims=True)


def _decay_logits(sm, alog_row, dtb_row):
    return -jnp.exp(alog_row) * jax.nn.softplus(sm + dtb_row)


def _const_spec(shape):
    nd = len(shape)
    return pl.BlockSpec(shape, lambda *_: (0,) * nd, pipeline_mode=pl.Buffered(1))


def _layer_spec(arr, l):
    shape = tuple(arr.shape[1:])
    return pl.BlockSpec((None,) + shape, lambda *_: (l,) + (0,) * len(shape),
                        pipeline_mode=pl.Buffered(1))


def _mod_kernel(c_ref, w_ref, b_ref, o_ref):
    c_act = jax.nn.silu(c_ref[...])
    o_ref[0] = _dot(c_act, w_ref[0]) + b_ref[0]


def _modulation(c_all, w_mod, b_mod):
    n = c_all.shape[0]
    tn = 1536
    return pl.pallas_call(
        _mod_kernel,
        grid=(DEPTH, (N_MOD * D_MODEL) // tn),
        in_specs=[
            pl.BlockSpec((n, D_MODEL), lambda l, j: (0, 0)),
            pl.BlockSpec((1, D_MODEL, tn), lambda l, j: (l, 0, j)),
            pl.BlockSpec((1, 1, tn), lambda l, j: (l, 0, j)),
        ],
        out_specs=pl.BlockSpec((1, n, tn), lambda l, j: (l, 0, j)),
        out_shape=jax.ShapeDtypeStruct((DEPTH, n, N_MOD * D_MODEL), F32),
        compiler_params=pltpu.CompilerParams(
            dimension_semantics=("arbitrary", "arbitrary"), vmem_limit_bytes=VMEM_LIMIT),
        name="modulation",
    )(c_all, w_mod, b_mod.reshape(DEPTH, 1, N_MOD * D_MODEL))


def _mixer_kernel(x_ref, mod_ref, norm1_ref, w_in_ref, convw_ref, alog_ref, dtb_ref,
                  gdn_ref, sgun_ref, wsgu_ref, bsgu_ref, w_out_ref,
                  xo_ref, s_out_ref, qkvbuf_ref,
                  xp_scr, s_scr, *, tb):
    t = pl.program_id(1)
    nt = pl.num_programs(1)

    @pl.when(t == 0)
    def _():
        xp_scr[0:SUBLANES, :] = jnp.zeros((SUBLANES, QKV_DIM), F32)
        s_scr[...] = jnp.zeros(s_scr.shape, F32)

    x = x_ref[0]
    sh1 = mod_ref[0, :, 0:D_MODEL]
    sc1 = mod_ref[0, :, D_MODEL:2 * D_MODEL]
    g1 = mod_ref[0, :, 2 * D_MODEL:3 * D_MODEL]
    h = _rms(x, norm1_ref[...]) * (1.0 + sc1) + sh1
    proj = _dot(h, w_in_ref[...])

    qkv = proj[:, 0:QKV_DIM]
    xp_scr[SUBLANES:SUBLANES + tb, :] = qkv
    acc = qkv * convw_ref[3:4, :]
    for j in range(CONV_A - 1):
        off = SUBLANES - (CONV_A - 1) + j
        acc = acc + xp_scr[off:off + tb, :] * convw_ref[j:j + 1, :]
    qkvbuf_ref[0] = xp_scr[tb + SUBLANES - (CONV_A - 1):tb + SUBLANES, :]
    xp_scr[0:SUBLANES, :] = xp_scr[tb:tb + SUBLANES, :]
    qkvc = jax.nn.silu(acc)

    sm = proj[:, P_SM:P_SM + LANES]
    beta_all = jax.nn.sigmoid(sm)
    g_all = _decay_logits(sm, alog_ref[...], dtb_ref[...])

    ri = lax.broadcasted_iota(jnp.int32, (tb, tb), 0)
    ci = lax.broadcasted_iota(jnp.int32, (tb, tb), 1)
    cum_mat = jnp.where((ri // DELTA_C == ci // DELTA_C) & (ri >= ci), 1.0, 0.0).astype(BF16)
    g_hi = g_all.astype(BF16)
    r1 = g_all - g_hi.astype(F32)
    g_mid = r1.astype(BF16)
    g_lo = (r1 - g_mid.astype(F32)).astype(BF16)
    gc_all = (jnp.dot(cum_mat, g_hi, preferred_element_type=F32)
              + jnp.dot(cum_mat, g_mid, preferred_element_type=F32)
              + jnp.dot(cum_mat, g_lo, preferred_element_type=F32))
    gc_t = gc_all.T

    rs = lax.broadcasted_iota(jnp.int32, (SUPER, SUPER), 0)
    cs = lax.broadcasted_iota(jnp.int32, (SUPER, SUPER), 1)
    same_chunk = (rs // DELTA_C) == (cs // DELTA_C)
    tril = same_chunk & (rs >= cs)
    strict = same_chunk & (rs > cs)
    eye = jnp.where(rs == cs, 1.0, 0.0)
    pair_mask = (rs // 2 == cs // 2) & (rs > cs)
    quad_masks = []
    half = 2
    while half < DELTA_C:
        quad_masks.append((rs // (2 * half) == cs // (2 * half))
                          & (rs % (2 * half) >= half) & (cs % (2 * half) < half))
        half *= 2
    chunks_per_super = SUPER // DELTA_C

    row_chunk = lax.broadcasted_iota(jnp.int32, (SUPER, 1), 0) // DELTA_C
    n_super = tb // SUPER

    pairs = [(hh, s) for s in range(n_super) for hh in range(H_A)]
    head_in = []
    for hh in range(H_A):
        head_in.append((
            _l2n(qkvc[:, hh * DK:(hh + 1) * DK]) * (DK ** -0.5),
            _l2n(qkvc[:, QK_DIM + hh * DK:QK_DIM + (hh + 1) * DK]),
            qkvc[:, 2 * QK_DIM + hh * DV:2 * QK_DIM + (hh + 1) * DV],
            _lane_col(beta_all, hh),
            _lane_col(gc_all, H_A + hh)))

    low_p, qk_p, rhs_p, qg_p, kdec_p, glast_p = {}, {}, {}, {}, {}, {}
    for p in pairs:
        hh, s = p
        q_h, k_h, v_h, beta_c, gc_c = head_in[hh]
        r0 = s * SUPER
        qs = q_h[r0:r0 + SUPER]
        ks = k_h[r0:r0 + SUPER]
        bs = beta_c[r0:r0 + SUPER]
        gcol = gc_c[r0:r0 + SUPER]
        grow = gc_t[H_A + hh:H_A + hh + 1, r0:r0 + SUPER]
        decay = jnp.exp(jnp.where(tril, gcol - grow, -jnp.inf))
        kb = ks * bs
        eg = jnp.exp(gcol)
        low_p[p] = jnp.where(strict, _dot_nt(kb, ks) * decay, 0.0)
        qk_p[p] = jnp.where(tril, _dot_nt(qs, ks) * decay, 0.0)
        rhs_p[p] = jnp.concatenate([v_h[r0:r0 + SUPER] * bs, kb * eg], axis=1)
        qg_p[p] = qs * eg
        g_last = [grow[:, (c + 1) * DELTA_C - 1:(c + 1) * DELTA_C] for c in range(chunks_per_super)]
        glast_col = g_last[-1]
        for c in range(chunks_per_super - 2, -1, -1):
            glast_col = jnp.where(row_chunk == c, g_last[c], glast_col)
        kdec_p[p] = ks * jnp.exp(glast_col - gcol)
        glast_p[p] = g_last

    inv_p = {p: eye - jnp.where(pair_mask, low_p[p], 0.0) for p in pairs}
    for qm in quad_masks:
        t_p = {p: _dot(inv_p[p], jnp.where(qm, low_p[p], 0.0)) for p in pairs}
        inv_p = {p: inv_p[p] - _dot(t_p[p], inv_p[p]) for p in pairs}
    sol_p = {p: _dot(inv_p[p], rhs_p[p]) for p in pairs}
    qs_p = {p: _dot(qk_p[p], sol_p[p]) for p in pairs}
    bm_p = {(p, c): _dot_tn(kdec_p[p][c * DELTA_C:(c + 1) * DELTA_C],
                            sol_p[p][c * DELTA_C:(c + 1) * DELTA_C])
            for p in pairs for c in range(chunks_per_super)}

    state = [s_scr[hh] for hh in range(H_A)]
    o_rows = [[] for _ in range(H_A)]
    for s in range(n_super):
        for c in range(chunks_per_super):
            c0 = c * DELTA_C
            for hh in range(H_A):
                p = (hh, s)
                s_b = state[hh].astype(BF16)
                q_eff = qg_p[p][c0:c0 + DELTA_C] - qs_p[p][c0:c0 + DELTA_C, DV:DV + DK]
                o_rows[hh].append(_dot(q_eff, s_b) + qs_p[p][c0:c0 + DELTA_C, 0:DV])
                bm = bm_p[(p, c)]
                state[hh] = (state[hh] * jnp.exp(glast_p[p][c]) - _dot(bm[:, DV:DV + DK], s_b)
                             + bm[:, 0:DV])
    o_heads = []
    for hh in range(H_A):
        s_scr[hh] = state[hh]
        o_h = jnp.concatenate(o_rows[hh], axis=0)
        z_h = proj[:, P_Z + hh * DV:P_Z + (hh + 1) * DV]
        o_heads.append(_rms(o_h, gdn_ref[...]) * jax.nn.silu(z_h))

    rc = lax.broadcasted_iota(jnp.int32, (SGU_CHUNK, SGU_CHUNK), 0)
    cc = lax.broadcasted_iota(jnp.int32, (SGU_CHUNK, SGU_CHUNK), 1)
    for hh in range(H_B):
        u_h = jax.nn.gelu(proj[:, P_U + hh * D_B:P_U + (hh + 1) * D_B])
        v_h = _rms(jax.nn.gelu(proj[:, P_VB + hh * D_B:P_VB + (hh + 1) * D_B]),
                   sgun_ref[:, hh * D_B:(hh + 1) * D_B])
        w_h = jnp.where(rc >= cc, wsgu_ref[hh], 0.0)
        b_col = _lane_col(bsgu_ref[...], hh)
        z_rows = []
        for s in range(tb // SGU_CHUNK):
            r0 = s * SGU_CHUNK
            z_rows.append(_dot(w_h, v_h[r0:r0 + SGU_CHUNK]) + b_col)
        o_heads.append(u_h * jnp.concatenate(z_rows, axis=0))

    mix = jnp.concatenate(o_heads, axis=1)
    xo_ref[0] = x + g1 * _dot(mix, w_out_ref[...])

    @pl.when(t == nt - 1)
    def _():
        s_out_ref[0] = s_scr[...]


def _prompt_mixer(l, x, mod, norm1, w_in, convw, alog_row, dtb_row, gdn, sgun, wsgu, bsgu_t, w_out):
    nb, seq, _ = x.shape
    tb = TB_MIX
    kern = functools.partial(_mixer_kernel, tb=tb)
    consts = (norm1, w_in, convw, alog_row, dtb_row, gdn, sgun, wsgu, bsgu_t, w_out)
    return pl.pallas_call(
        kern,
        grid=(nb, seq // tb),
        in_specs=[
            pl.BlockSpec((1, tb, D_MODEL), lambda b, t: (b, t, 0)),
            pl.BlockSpec((None, 1, 1, N_MOD * D_MODEL), lambda b, t: (l, b, 0, 0)),
        ] + [_layer_spec(a, l) for a in consts],
        out_specs=[
            pl.BlockSpec((1, tb, D_MODEL), lambda b, t: (b, t, 0)),
            pl.BlockSpec((1, H_A, DK, DV), lambda b, t: (b, 0, 0, 0)),
            pl.BlockSpec((1, CONV_A - 1, QKV_DIM), lambda b, t: (b, 0, 0)),
        ],
        out_shape=[
            jax.ShapeDtypeStruct((nb, seq, D_MODEL), F32),
            jax.ShapeDtypeStruct((nb, H_A, DK, DV), F32),
            jax.ShapeDtypeStruct((nb, CONV_A - 1, QKV_DIM), F32),
        ],
        scratch_shapes=[
            pltpu.VMEM((tb + SUBLANES, QKV_DIM), F32),
            pltpu.VMEM((H_A, DK, DV), F32),
        ],
        compiler_params=pltpu.CompilerParams(
            dimension_semantics=("arbitrary", "arbitrary"), vmem_limit_bytes=VMEM_LIMIT),
        name="prompt_mixer",
    )(x, mod, norm1, w_in, convw, alog_row, dtb_row, gdn, sgun, wsgu, bsgu_t, w_out)


def _ffn_body(x, mod_row, norm2, w_up_ref, upc_fn, w_down_ref):
    sh2 = mod_row[:, 3 * D_MODEL:4 * D_MODEL]
    sc2 = mod_row[:, 4 * D_MODEL:5 * D_MODEL]
    g2 = mod_row[:, 5 * D_MODEL:6 * D_MODEL]
    h2 = _rms(x, norm2) * (1.0 + sc2) + sh2
    up = _dot(h2, w_up_ref[...])
    upc = upc_fn(up)
    gated = jax.nn.silu(upc[:, 0:D_FF]) * upc[:, D_FF:2 * D_FF]
    return x + g2 * _dot(gated, w_down_ref[...])


def _ffn_kernel(x_ref, mod_ref, norm2_ref, w_up_ref, convw_ref, convb_ref, w_down_ref, fnorm_ref,
                xo_ref, ffnbuf_ref, xp_scr, *, tb, final):
    t = pl.program_id(1)

    @pl.when(t == 0)
    def _():
        xp_scr[0:SUBLANES, :] = jnp.zeros((SUBLANES, 2 * D_FF), F32)

    def conv(up):
        xp_scr[SUBLANES:SUBLANES + tb, :] = up
        acc = up * convw_ref[CONV_F - 1:CONV_F, :] + convb_ref[...]
        for j in range(CONV_F - 1):
            off = SUBLANES - (CONV_F - 1) + j
            acc = acc + xp_scr[off:off + tb, :] * convw_ref[j:j + 1, :]
        ffnbuf_ref[0] = xp_scr[tb + SUBLANES - (CONV_F - 1):tb + SUBLANES, :]
        xp_scr[0:SUBLANES, :] = xp_scr[tb:tb + SUBLANES, :]
        return acc

    x_new = _ffn_body(x_ref[0], mod_ref[0], norm2_ref[...], w_up_ref, conv, w_down_ref)
    if final:
        x_new = _rms(x_new, fnorm_ref[...])
    xo_ref[0] = x_new


def _prompt_ffn(l, x, mod, norm2, w_up, convw, convb, w_down, fnorm, final):
    nb, seq, _ = x.shape
    tb = TB_FFN
    kern = functools.partial(_ffn_kernel, tb=tb, final=final)
    return pl.pallas_call(
        kern,
        grid=(nb, seq // tb),
        in_specs=[
            pl.BlockSpec((1, tb, D_MODEL), lambda b, t: (b, t, 0)),
            pl.BlockSpec((None, 1, 1, N_MOD * D_MODEL), lambda b, t: (l, b, 0, 0)),
        ] + [_layer_spec(a, l) for a in (norm2, w_up, convw, convb, w_down)] + [
            _const_spec((1, D_MODEL)),
        ],
        out_specs=[
            pl.BlockSpec((1, tb, D_MODEL), lambda b, t: (b, t, 0)),
            pl.BlockSpec((1, CONV_F - 1, 2 * D_FF), lambda b, t: (b, 0, 0)),
        ],
        out_shape=[
            jax.ShapeDtypeStruct((nb, seq, D_MODEL), F32),
            jax.ShapeDtypeStruct((nb, CONV_F - 1, 2 * D_FF), F32),
        ],
        scratch_shapes=[pltpu.VMEM((tb + SUBLANES, 2 * D_FF), F32)],
        compiler_params=pltpu.CompilerParams(
            dimension_semantics=("arbitrary", "arbitrary"), vmem_limit_bytes=VMEM_LIMIT),
        name="prompt_ffn",
    )(x, mod, norm2, w_up, convw, convb, w_down, fnorm)


def _bcast_heads(cols):
    return jnp.concatenate([jnp.broadcast_to(c, (c.shape[0], LANES)) for c in cols], axis=1)


def _sample_in_kernel(x_ref, mod_ref, norm1_ref, w_in_ref, cstate_ref, convw_ref, alog_ref, dtb_ref,
                      sgun_ref, wd_ref, b0_ref,
                      ke_ref, qe_ref, kn_ref, v_ref, beta_ref, eg_ref, qk_ref, z_ref, ob_ref, vv_ref,
                      cnew_ref):
    x = x_ref[...]
    sh1 = mod_ref[:, 0:D_MODEL]
    sc1 = mod_ref[:, D_MODEL:2 * D_MODEL]
    h = _rms(x, norm1_ref[...]) * (1.0 + sc1) + sh1
    proj = _dot(h, w_in_ref[...])
    qkv = proj[:, 0:QKV_DIM]
    acc = qkv * convw_ref[CONV_A - 1:CONV_A, :]
    for j in range(CONV_A - 1):
        acc = acc + cstate_ref[j] * convw_ref[j:j + 1, :]
    cnew_ref[...] = qkv
    qkvc = jax.nn.silu(acc)

    sm = proj[:, P_SM:P_SM + LANES]
    beta_all = jax.nn.sigmoid(sm)
    eg_all = jnp.exp(_decay_logits(sm, alog_ref[...], dtb_ref[...]))
    beta_cols, eg_cols, qk_cols = [], [], []
    for hh in range(H_A):
        q_h = _l2n(qkvc[:, hh * DK:(hh + 1) * DK]) * (DK ** -0.5)
        k_h = _l2n(qkvc[:, QK_DIM + hh * DK:QK_DIM + (hh + 1) * DK])
        eg_c = _lane_col(eg_all, H_A + hh)
        sl = slice(hh * DK, (hh + 1) * DK)
        ke_ref[:, sl] = k_h * eg_c
        qe_ref[:, sl] = q_h * eg_c
        kn_ref[:, sl] = k_h
        beta_cols.append(_lane_col(beta_all, hh))
        eg_cols.append(eg_c)
        qk_cols.append(jnp.sum(q_h * k_h, axis=1, keepdims=True))
    v_ref[...] = qkvc[:, 2 * QK_DIM:2 * QK_DIM + V_DIM]
    beta_ref[...] = _bcast_heads(beta_cols)
    eg_ref[...] = _bcast_heads(eg_cols)
    qk_ref[...] = _bcast_heads(qk_cols)
    z_ref[...] = proj[:, P_Z:P_Z + V_DIM]

    vv_parts = []
    for hh in range(H_B):
        vv_parts.append(_rms(jax.nn.gelu(proj[:, P_VB + hh * D_B:P_VB + (hh + 1) * D_B]),
                             sgun_ref[:, hh * D_B:(hh + 1) * D_B]))
    vv = jnp.concatenate(vv_parts, axis=1)
    vv_ref[...] = vv
    ob_ref[...] = jax.nn.gelu(proj[:, P_U:P_U + SGU_DIM]) * (wd_ref[...] * vv + b0_ref[...])


def _sample_in(l, x, mod, norm1, w_in, cstate, convw, alog_row, dtb_row, sgun, wd_row, b0_row):
    n = x.shape[0]
    wide = jax.ShapeDtypeStruct((n, V_DIM), F32)
    stacked = (mod, norm1, w_in, cstate, convw, alog_row, dtb_row, sgun, wd_row, b0_row)
    outs = [wide] * 10 + [jax.ShapeDtypeStruct((n, QKV_DIM), F32)]
    return pl.pallas_call(
        _sample_in_kernel,
        grid=(1,),
        in_specs=[_const_spec(x.shape)] + [_layer_spec(a, l) for a in stacked],
        out_specs=[pl.BlockSpec(o.shape, lambda i: (0, 0)) for o in outs],
        out_shape=outs,
        compiler_params=pltpu.CompilerParams(
            dimension_semantics=("arbitrary",), vmem_limit_bytes=VMEM_LIMIT),
        name="sample_in",
    )(x, *stacked)


def _sample_state_kernel(s_ref, ke_ref, qe_ref, kn_ref, v_ref, beta_ref, eg_ref, qk_ref, *rest):
    s_out_ref, o_ref = rest[-2:]
    bb = BB_STATE
    row = lax.broadcasted_iota(jnp.int32, (bb, DK), 0)
    v_new_h = []
    for hh in range(H_A):
        sl = slice(hh * DK, (hh + 1) * DK)
        lhs = jnp.concatenate([ke_ref[:, sl], qe_ref[:, sl]], axis=0)
        pred_k = jnp.zeros((bb, DV), F32)
        pred_q = jnp.zeros((bb, DV), F32)
        for i in range(bb):
            r = jnp.dot(lhs, s_ref[i, hh], preferred_element_type=F32)
            pred_k = pred_k + jnp.where(row == i, r[0:bb], 0.0)
            pred_q = pred_q + jnp.where(row == i, r[bb:2 * bb], 0.0)
        v_new = beta_ref[:, sl] * (v_ref[:, sl] - pred_k)
        o_ref[:, sl] = pred_q + qk_ref[:, sl] * v_new
        v_new_h.append(v_new)
    for hh in range(H_A):
        sl = slice(hh * DK, (hh + 1) * DK)
        kn = kn_ref[:, sl]
        eg = eg_ref[:, sl]
        for i in range(bb):
            upd = lax.dot_general(jnp.where(row == i, kn, 0.0), v_new_h[hh], (((0,), (0,)), ((), ())),
                                  preferred_element_type=F32)
            s_out_ref[i, hh] = s_ref[i, hh] * eg[i:i + 1, :] + upd


def _sample_state(l, s_all, s_new_all, ke, qe, kn, v, beta, eg, qk):
    n = s_all.shape[1]
    bb = BB_STATE
    row_spec = pl.BlockSpec((bb, V_DIM), lambda i: (i, 0))
    s_spec = pl.BlockSpec((None, bb, H_A, DK, DV), lambda i: (l, i, 0, 0, 0))
    ins = [s_all, ke, qe, kn, v, beta, eg, qk]
    in_specs = [s_spec] + [row_spec] * 7
    aliases = {}
    if s_new_all is not None:
        ins.append(s_new_all)
        in_specs.append(pl.BlockSpec(memory_space=pl.ANY))
        aliases = {len(ins) - 1: 0}
    return pl.pallas_call(
        _sample_state_kernel,
        grid=(n // bb,),
        in_specs=in_specs,
        out_specs=[s_spec, row_spec],
        out_shape=[jax.ShapeDtypeStruct(s_all.shape, F32), jax.ShapeDtypeStruct((n, V_DIM), F32)],
        input_output_aliases=aliases,
        compiler_params=pltpu.CompilerParams(
            dimension_semantics=("arbitrary",), vmem_limit_bytes=VMEM_LIMIT),
        name="sample_state",
    )(*ins)


def _sample_out_kernel(x_ref, mod_ref, o_ref, z_ref, ob_ref, gdn_ref, w_out_ref, norm2_ref, w_up_ref,
                       fstate_ref, convw_ref, convb_ref, w_down_ref, fnorm_ref,
                       xo_ref, fnew_ref, *, final):
    x = x_ref[...]
    g1 = mod_ref[:, 2 * D_MODEL:3 * D_MODEL]
    parts = []
    for hh in range(H_A):
        sl = slice(hh * DV, (hh + 1) * DV)
        parts.append(_rms(o_ref[:, sl], gdn_ref[...]) * jax.nn.silu(z_ref[:, sl]))
    parts.append(ob_ref[...])
    mix = jnp.concatenate(parts, axis=1)
    x = x + g1 * _dot(mix, w_out_ref[...])

    def conv(up):
        acc = up * convw_ref[CONV_F - 1:CONV_F, :] + convb_ref[...]
        for j in range(CONV_F - 1):
            acc = acc + fstate_ref[j] * convw_ref[j:j + 1, :]
        fnew_ref[...] = up
        return acc

    x_new = _ffn_body(x, mod_ref[...], norm2_ref[...], w_up_ref, conv, w_down_ref)
    if final:
        x_new = _rms(x_new, fnorm_ref[...])
    xo_ref[...] = x_new


def _sample_out(l, x, mod, o, z, ob, gdn, w_out, norm2, w_up, fstate, convw, convb, w_down, fnorm,
                final):
    n = x.shape[0]
    ins = (x, mod, o, z, ob, gdn, w_out, norm2, w_up, fstate, convw, convb, w_down, fnorm)
    per_layer = (False, True, False, False, False, True, True, True, True, True, True, True, True, False)
    outs = [jax.ShapeDtypeStruct((n, D_MODEL), F32),
            jax.ShapeDtypeStruct((n, 2 * D_FF), F32)]
    return pl.pallas_call(
        functools.partial(_sample_out_kernel, final=final),
        grid=(1,),
        in_specs=[_layer_spec(a, l) if st else _const_spec(a.shape) for a, st in zip(ins, per_layer)],
        out_specs=[pl.BlockSpec(o_.shape, lambda i: (0, 0)) for o_ in outs],
        out_shape=outs,
        compiler_params=pltpu.CompilerParams(
            dimension_semantics=("arbitrary",), vmem_limit_bytes=VMEM_LIMIT),
        name="sample_out",
    )(*ins)


def _lane_rows(vals, offset):
    return jnp.zeros((vals.shape[0], 1, LANES), F32).at[:, 0, offset:offset + vals.shape[1]].set(vals)


def kernel(x_prompt, x_sample, state_delta, state_qkv_conv, state_ffn_conv, c_prompt, c_sample,
           w_mod, b_mod, norm1, w_in, conv_qkv, a_log, dt_bias, gdn_norm, sgu_norm, w_sgu, b_sgu,
           w_out, norm2, w_up, conv_ffn_w, conv_ffn_b, w_down, final_norm):
    nbp = x_prompt.shape[0]
    nbs = x_sample.shape[0]

    off_z = QKV_DIM
    off_beta = off_z + V_DIM
    off_u = off_beta + 2 * H_A
    off_vb = off_u + SGU_DIM
    w_in_p = jnp.concatenate(
        [w_in[:, :, 0:off_beta], w_in[:, :, off_u:off_vb + SGU_DIM], w_in[:, :, off_beta:off_u],
         jnp.zeros((DEPTH, D_MODEL, LANES - 2 * H_A), w_in.dtype)], axis=2).astype(BF16)
    w_out_b = w_out.astype(BF16)
    w_up_b = w_up.astype(BF16)
    w_down_b = w_down.astype(BF16)

    mod_all = _modulation(jnp.concatenate([c_prompt, c_sample], axis=0), w_mod, b_mod)
    mod_p = mod_all[:, :nbp].reshape(DEPTH, nbp, 1, N_MOD * D_MODEL)
    mod_s = mod_all[:, nbp:]
    fnorm = final_norm.reshape(1, D_MODEL)

    n1 = norm1.reshape(DEPTH, 1, D_MODEL)
    n2 = norm2.reshape(DEPTH, 1, D_MODEL)
    alog_rows = _lane_rows(a_log, H_A)
    dtb_rows = _lane_rows(dt_bias, H_A)
    gdn = gdn_norm.reshape(DEPTH, 1, DV)
    sgun = sgu_norm.reshape(DEPTH, 1, SGU_DIM)
    convb = conv_ffn_b.reshape(DEPTH, 1, 2 * D_FF)
    bsgu_t = jnp.zeros((DEPTH, SGU_CHUNK, LANES), F32).at[:, :, 0:H_B].set(
        jnp.swapaxes(b_sgu, 1, 2))
    wd_rows = jnp.repeat(w_sgu[:, :, 0, 0], D_B, axis=1).reshape(DEPTH, 1, SGU_DIM)
    b0_rows = jnp.repeat(b_sgu[:, :, 0], D_B, axis=1).reshape(DEPTH, 1, SGU_DIM)

    xp = x_prompt
    xs = x_sample.reshape(nbs, D_MODEL)
    cstate = jnp.swapaxes(state_qkv_conv, 1, 2)
    fstate = jnp.swapaxes(state_ffn_conv, 1, 2)
    delta_s = jnp.zeros_like(state_delta)
    delta_p, qkv_p, qkv_s, ffn_p, ffn_s, vv_s = [], [], [], [], [], []
    for l in range(DEPTH):
        final = l == DEPTH - 1
        xp, s_p, qb_p = _prompt_mixer(l, xp, mod_p, n1, w_in_p, conv_qkv, alog_rows, dtb_rows,
                                      gdn, sgun, w_sgu, bsgu_t, w_out_b)
        xp, fb_p = _prompt_ffn(l, xp, mod_p, n2, w_up_b, conv_ffn_w, convb, w_down_b, fnorm, final)
        delta_p.append(s_p)
        qkv_p.append(qb_p)
        ffn_p.append(fb_p)

        (ke, qe, kn, v, beta, eg, qk, z, ob, vv, c_new) = _sample_in(
            l, xs, mod_s, n1, w_in_p, cstate, conv_qkv, alog_rows, dtb_rows, sgun, wd_rows, b0_rows)
        delta_s, o = _sample_state(l, state_delta, delta_s, ke, qe, kn, v, beta, eg, qk)
        xs, f_new = _sample_out(l, xs, mod_s, o, z, ob, gdn, w_out_b, n2, w_up_b, fstate,
                                conv_ffn_w, convb, w_down_b, fnorm, final)
        qkv_s.append(c_new)
        ffn_s.append(f_new)
        vv_s.append(vv.reshape(nbs, 1, H_B, D_B))

    qkv_s_all = jnp.concatenate([state_qkv_conv[:, :, 1:, :], jnp.stack(qkv_s)[:, :, None, :]], axis=2)
    ffn_s_all = jnp.concatenate([state_ffn_conv[:, :, 1:, :], jnp.stack(ffn_s)[:, :, None, :]], axis=2)
    return (xp, xs.reshape(nbs, 1, D_MODEL), jnp.stack(delta_p), delta_s,
            jnp.stack(qkv_p), qkv_s_all, jnp.stack(ffn_p), ffn_s_all,
            jnp.stack(vv_s))
```
